```python
import math
import jax, jax.numpy as jnp
from jax import lax
import numpy as np

D_MODEL = 2048
BATCH = 2
SEQ = 4096
DEPTH = 2
DEC_BATCH = 128
DEC_SEQ = 8
PAST_LEN = 8192
PAGE_SIZE = 128

N_META = 16
N_HEADS = 16
N_KV_HEADS = 4
N_GROUP = N_HEADS // N_KV_HEADS
HEAD_DIM = 64
ATTN_WIDTH = N_HEADS * HEAD_DIM
KV_WIDTH = N_KV_HEADS * HEAD_DIM
WINDOW = 128
BLOCK = 128
CONV_WIDTH = D_MODEL // 2
CONV_K = 31
LRU_WIDTH = D_MODEL // 2
LRU_BLOCKS = 16
LRU_BLOCK_DIM = LRU_WIDTH // LRU_BLOCKS
LRU_CONV_K = 4
LRU_C = 8.0
N_BRANCH = 3
OFF_K = ATTN_WIDTH
OFF_V = OFF_K + KV_WIDTH
OFF_CONV = OFF_V + KV_WIDTH
OFF_LRU_X = OFF_CONV + 2 * CONV_WIDTH
OFF_LRU_G = OFF_LRU_X + LRU_WIDTH
OFF_GATES = OFF_LRU_G + LRU_WIDTH
IN_WIDTH = OFF_GATES + N_BRANCH * D_MODEL
IN_SPLITS = [OFF_K, OFF_V, OFF_CONV, OFF_LRU_X, OFF_LRU_G, OFF_GATES]
D_FF = 5632
N_EXPERTS = 8
TOP_K = 2
D_FF_EXPERT = 5632
N_DENSE = (DEPTH + 1) // 2
N_MOE = DEPTH // 2
EPS = 1e-6
NEG_INF = -1e30

kernel_name = "hybrid_swa_conformer_rglru_step"


def rms_norm(x, g):
    xf = x.astype(jnp.float32)
    y = xf * lax.rsqrt(jnp.mean(xf * xf, axis=-1, keepdims=True) + EPS)
    return (y * g.astype(jnp.float32)).astype(x.dtype)


def layer_norm(x, g, b):
    xf = x.astype(jnp.float32)
    mu = jnp.mean(xf, axis=-1, keepdims=True)
    var = jnp.mean(jnp.square(xf - mu), axis=-1, keepdims=True)
    y = (xf - mu) * lax.rsqrt(var + EPS) * g.astype(jnp.float32) + b.astype(jnp.float32)
    return y.astype(x.dtype)


def alibi_slopes():
    return jnp.exp2(-8.0 * (jnp.arange(N_HEADS, dtype=jnp.float32) + 1.0) / N_HEADS)


def sink_attention(q, k, v, q_pos, k_pos, sinks):
    s = jnp.einsum("...qhgd,...khd->...hgqk", q, k, preferred_element_type=jnp.float32) * (HEAD_DIM ** -0.5)
    dist = (q_pos[..., :, None] - k_pos[..., None, :]).astype(jnp.float32)
    allowed = (dist >= 0) & (dist <= WINDOW) & (k_pos[..., None, :] >= 0)
    slopes = alibi_slopes().reshape(N_KV_HEADS, N_GROUP, 1, 1)
    s = jnp.where(allowed[..., None, None, :, :], s - slopes * dist[..., None, None, :, :], NEG_INF)
    sink = jnp.broadcast_to(sinks.astype(jnp.float32).reshape(N_KV_HEADS, N_GROUP, 1, 1), s.shape[:-1] + (1,))
    p = jax.nn.softmax(jnp.concatenate([s, sink], axis=-1), axis=-1)[..., :-1]
    return jnp.einsum("...hgqk,...khd->...qhgd", p.astype(v.dtype), v)


def attention_prompt(q, k, v, sinks):
    B, L = q.shape[0], q.shape[1]
    pad = (-N_META) % BLOCK
    Lp = L + pad
    nb = Lp // BLOCK
    qb = jnp.pad(q, ((0, 0), (pad, 0), (0, 0), (0, 0), (0, 0))).reshape(B, nb, BLOCK, N_KV_HEADS, N_GROUP, HEAD_DIM)

    def band(t):
        tb = jnp.pad(t, ((0, 0), (pad, 0), (0, 0), (0, 0))).reshape(B, nb, BLOCK, N_KV_HEADS, HEAD_DIM)
        prev = jnp.pad(tb, ((0, 0), (1, 0), (0, 0), (0, 0), (0, 0)))[:, :-1]
        return jnp.concatenate([prev, tb], axis=2)

    pos_b = (jnp.arange(Lp) - pad).reshape(nb, BLOCK)
    k_pos = jnp.concatenate([pos_b - BLOCK, pos_b], axis=1)
    o = sink_attention(qb, band(k), band(v), pos_b, k_pos, sinks)
    return o.reshape(B, Lp, ATTN_WIDTH)[:, pad:]


def attention_sample(q, k, v, ck, cv, sinks):
    T, wb = q.shape[1], ck.shape[1]
    k_all = jnp.concatenate([ck.astype(k.dtype), k], axis=1)
    v_all = jnp.concatenate([cv.astype(v.dtype), v], axis=1)
    q_pos = PAST_LEN + jnp.arange(T)
    k_pos = PAST_LEN - wb + jnp.arange(wb + T)
    o = sink_attention(q, k_all, v_all, q_pos, k_pos, sinks)
    return o.reshape(q.shape[0], T, ATTN_WIDTH), k_all[:, -wb:], v_all[:, -wb:]


def depthwise_causal_conv(x_ext, w, b):
    y = lax.conv_general_dilated(x_ext, w[:, None, :].astype(x_ext.dtype), window_strides=(1,), padding="VALID",
                                 dimension_numbers=("NWC", "WIO", "NWC"), feature_group_count=x_ext.shape[-1])
    return y + b


def conv_branch(u2, buf, w_dw, b_dw, ln_g, ln_b):
    a, gate = jnp.split(u2, 2, axis=-1)
    u = a * jax.nn.sigmoid(gate)
    ext = jnp.concatenate([buf.astype(u.dtype), u], axis=1)
    y = jax.nn.silu(layer_norm(depthwise_causal_conv(ext, w_dw, b_dw), ln_g, ln_b))
    return y, ext[:, -(CONV_K - 1):]


def _linear_combine(left, right):
    a1, b1 = left
    a2, b2 = right
    return a1 * a2, a2 * b1 + b2


def rglru_branch(xb, gb, conv_buf, h0, conv_w, conv_b, w_r, b_r, w_i, b_i, lam):
    B, T, _ = xb.shape
    ext = jnp.concatenate([conv_buf.astype(xb.dtype), xb], axis=1)
    xc = depthwise_causal_conv(ext, conv_w, conv_b)
    xh = xc.reshape(B, T, LRU_BLOCKS, LRU_BLOCK_DIM)
    r = jax.nn.sigmoid(jnp.einsum("btnd,nde->btne", xh, w_r).reshape(B, T, LRU_WIDTH) + b_r)
    i = jax.nn.sigmoid(jnp.einsum("btnd,nde->btne", xh, w_i).reshape(B, T, LRU_WIDTH) + b_i)
    log_a = -LRU_C * r.astype(jnp.float32) * jax.nn.softplus(-lam.astype(jnp.float32))
    a = jnp.exp(log_a)
    bx = jnp.sqrt(-jnp.expm1(2.0 * log_a)) * (i * xc).astype(jnp.float32)
    bx = bx.at[:, 0].add(a[:, 0] * h0.astype(jnp.float32))
    _, h = lax.associative_scan(_linear_combine, (a, bx), axis=1)
    y = h.astype(xb.dtype) * jax.nn.gelu(gb)
    return y, ext[:, -(LRU_CONV_K - 1):], h[:, -1]


def swiglu(x, w1, w3, w2):
    return (jax.nn.silu(x @ w1) * (x @ w3)) @ w2


def moe_swiglu(x, w_router, b_router, w1, w3, w2):
    logits = (x @ w_router).astype(jnp.float32) + b_router.astype(jnp.float32)
    top_v, top_i = lax.top_k(logits, TOP_K)
    top_w = jax.nn.softmax(top_v, axis=-1)
    combine = jnp.sum(jax.nn.one_hot(top_i, N_EXPERTS, dtype=jnp.float32) * top_w[..., None], axis=-2)
    out = jnp.zeros_like(x)
    for e in range(N_EXPERTS):
        out = out + combine[..., e:e + 1].astype(x.dtype) * swiglu(x, w1[e], w3[e], w2[e])
    return out


def mix_layer(xn, l, attn_state, conv_buf, lru_buf, h0, P):
    B, T, _ = xn.shape
    z = xn @ P["w_in"][l]
    q, k, v, u2, xb, gb, gl = jnp.split(z, IN_SPLITS, axis=-1)
    q = q.reshape(B, T, N_KV_HEADS, N_GROUP, HEAD_DIM)
    k = k.reshape(B, T, N_KV_HEADS, HEAD_DIM)
    v = v.reshape(B, T, N_KV_HEADS, HEAD_DIM)
    if attn_state is None:
        o = attention_prompt(q, k, v, P["attn_sinks"][l])
        keep = min(WINDOW, T)
        nk, nv = k[:, -keep:], v[:, -keep:]
    else:
        o, nk, nv = attention_sample(q, k, v, attn_state[0], attn_state[1], P["attn_sinks"][l])
    a_out = o @ P["w_attn_out"][l]
    c, nc = conv_branch(u2, conv_buf, P["conv_dw_w"][l], P["conv_dw_b"][l], P["conv_ln_g"][l], P["conv_ln_b"][l])
    c_out = c @ P["w_conv_out"][l]
    rr, nlc, nh = rglru_branch(xb, gb, lru_buf, h0, P["lru_conv_w"][l], P["lru_conv_b"][l], P["lru_w_r"][l],
                               P["lru_b_r"][l], P["lru_w_i"][l], P["lru_b_i"][l], P["lru_lambda"][l])
    r_out = rr @ P["w_lru_out"][l]
    g = jax.nn.sigmoid(gl).reshape(B, T, N_BRANCH, D_MODEL)
    merged = g[:, :, 0] * a_out + g[:, :, 1] * c_out + g[:, :, 2] * r_out
    return merged @ P["w_mix_out"][l], (nk, nv, nc, nlc, nh)


def trunk(x, states, P):
    B = x.shape[0]
    new = ([], [], [], [], [])
    for l in range(DEPTH):
        xn = rms_norm(x, P["norm_mix"][l])
        if states is None:
            attn_state = None
            conv_buf = jnp.zeros((B, CONV_K - 1, CONV_WIDTH), x.dtype)
            lru_buf = jnp.zeros((B, LRU_CONV_K - 1, LRU_WIDTH), x.dtype)
            h0 = jnp.zeros((B, LRU_WIDTH), jnp.float32)
        else:
            ck, cv, sc, slc, sh = states
            attn_state = (ck[l], cv[l])
            conv_buf, lru_buf, h0 = sc[l], slc[l], sh[l]
        m, st = mix_layer(xn, l, attn_state, conv_buf, lru_buf, h0, P)
        x = x + m
        hn = rms_norm(x, P["norm_ffn"][l])
        if l % 2 == 0:
            j = l // 2
            f = swiglu(hn, P["ffn_w1"][j], P["ffn_w3"][j], P["ffn_w2"][j])
        else:
            j = l // 2
            f = moe_swiglu(hn, P["moe_router_w"][j], P["moe_router_b"][j], P["moe_w1"][j], P["moe_w3"][j], P["moe_w2"][j])
        x = x + f
        for lst, s in zip(new, st):
            lst.append(s)
    y = rms_norm(x, P["norm_final"])
    return y, [jnp.stack(lst) for lst in new]


def setup_inputs(seed: int = 0) -> dict:
    key = jax.random.key(seed)
    ks = iter(jax.random.split(key, 48))
    f32 = jnp.float32

    def nrm(shape, scale):
        return jax.random.normal(next(ks), shape, f32) * scale

    wb = min(WINDOW, PAST_LEN)
    u = jax.random.uniform(next(ks), (DEPTH, LRU_WIDTH), f32, 0.9, 0.999)
    s = u ** (1.0 / LRU_C)
    lru_lambda = jnp.log(s) - jnp.log1p(-s)
    return {
        "x_prompt": nrm((BATCH, SEQ, D_MODEL), 1.0),
        "x_sample": nrm((DEC_BATCH, DEC_SEQ, D_MODEL), 1.0),
        "cache_attn_k": nrm((DEPTH, DEC_BATCH, wb, N_KV_HEADS, HEAD_DIM), 1.0),
        "cache_attn_v": nrm((DEPTH, DEC_BATCH, wb, N_KV_HEADS, HEAD_DIM), 1.0),
        "state_conv": nrm((DEPTH, DEC_BATCH, CONV_K - 1, CONV_WIDTH), 0.5),
        "state_lru_conv": nrm((DEPTH, DEC_BATCH, LRU_CONV_K - 1, LRU_WIDTH), 1.0),
        "state_lru_h": nrm((DEPTH, DEC_BATCH, LRU_WIDTH), 0.5),
        "meta_tokens": nrm((N_META, D_MODEL), 1.0),
        "norm_mix": 1.0 + nrm((DEPTH, D_MODEL), 0.02),
        "norm_ffn": 1.0 + nrm((DEPTH, D_MODEL), 0.02),
        "norm_final": 1.0 + nrm((D_MODEL,), 0.02),
        "w_in": nrm((DEPTH, D_MODEL, IN_WIDTH), D_MODEL ** -0.5),
        "attn_sinks": nrm((DEPTH, N_HEADS), 0.5),
        "w_attn_out": nrm((DEPTH, ATTN_WIDTH, D_MODEL), ATTN_WIDTH ** -0.5),
        "conv_dw_w": nrm((DEPTH, CONV_K, CONV_WIDTH), CONV_K ** -0.5),
        "conv_dw_b": nrm((DEPTH, CONV_WIDTH), 0.02),
        "conv_ln_g": 1.0 + nrm((DEPTH, CONV_WIDTH), 0.02),
        "conv_ln_b": nrm((DEPTH, CONV_WIDTH), 0.02),
        "w_conv_out": nrm((DEPTH, CONV_WIDTH, D_MODEL), CONV_WIDTH ** -0.5),
        "lru_conv_w": nrm((DEPTH, LRU_CONV_K, LRU_WIDTH), LRU_CONV_K ** -0.5),
        "lru_conv_b": nrm((DEPTH, LRU_WIDTH), 0.02),
        "lru_w_r": nrm((DEPTH, LRU_BLOCKS, LRU_BLOCK_DIM, LRU_BLOCK_DIM), LRU_BLOCK_DIM ** -0.5),
        "lru_b_r": nrm((DEPTH, LRU_WIDTH), 0.02),
        "lru_w_i": nrm((DEPTH, LRU_BLOCKS, LRU_BLOCK_DIM, LRU_BLOCK_DIM), LRU_BLOCK_DIM ** -0.5),
        "lru_b_i": nrm((DEPTH, LRU_WIDTH), 0.02),
        "lru_lambda": lru_lambda,
        "w_lru_out": nrm((DEPTH, LRU_WIDTH, D_MODEL), LRU_WIDTH ** -0.5),
        "w_mix_out": nrm((DEPTH, D_MODEL, D_MODEL), D_MODEL ** -0.5),
        "ffn_w1": nrm((N_DENSE, D_MODEL, D_FF), D_MODEL ** -0.5),
        "ffn_w3": nrm((N_DENSE, D_MODEL, D_FF), D_MODEL ** -0.5),
        "ffn_w2": nrm((N_DENSE, D_FF, D_MODEL), D_FF ** -0.5),
        "moe_router_w": nrm((N_MOE, D_MODEL, N_EXPERTS), D_MODEL ** -0.5),
        "moe_router_b": nrm((N_MOE, N_EXPERTS), 0.01),
        "moe_w1": nrm((N_MOE, N_EXPERTS, D_MODEL, D_FF_EXPERT), D_MODEL ** -0.5),
        "moe_w3": nrm((N_MOE, N_EXPERTS, D_MODEL, D_FF_EXPERT), D_MODEL ** -0.5),
        "moe_w2": nrm((N_MOE, N_EXPERTS, D_FF_EXPERT, D_MODEL), D_FF_EXPERT ** -0.5),
    }


def reference(x_prompt, x_sample, cache_attn_k, cache_attn_v, state_conv, state_lru_conv, state_lru_h,
              meta_tokens, norm_mix, norm_ffn, norm_final, w_in, attn_sinks, w_attn_out,
              conv_dw_w, conv_dw_b, conv_ln_g, conv_ln_b, w_conv_out,
              lru_conv_w, lru_conv_b, lru_w_r, lru_b_r, lru_w_i, lru_b_i, lru_lambda, w_lru_out,
              w_mix_out, ffn_w1, ffn_w3, ffn_w2, moe_router_w, moe_router_b, moe_w1, moe_w3, moe_w2):
    P = dict(norm_mix=norm_mix, norm_ffn=norm_ffn, norm_final=norm_final, w_in=w_in, attn_sinks=attn_sinks,
             w_attn_out=w_attn_out, conv_dw_w=conv_dw_w, conv_dw_b=conv_dw_b, conv_ln_g=conv_ln_g,
             conv_ln_b=conv_ln_b, w_conv_out=w_conv_out, lru_conv_w=lru_conv_w, lru_conv_b=lru_conv_b,
             lru_w_r=lru_w_r, lru_b_r=lru_b_r, lru_w_i=lru_w_i, lru_b_i=lru_b_i, lru_lambda=lru_lambda,
             w_lru_out=w_lru_out, w_mix_out=w_mix_out, ffn_w1=ffn_w1, ffn_w3=ffn_w3, ffn_w2=ffn_w2,
             moe_router_w=moe_router_w, moe_router_b=moe_router_b, moe_w1=moe_w1, moe_w3=moe_w3, moe_w2=moe_w2)
    B = x_prompt.shape[0]
    meta = jnp.broadcast_to(meta_tokens.astype(x_prompt.dtype)[None], (B, N_META, D_MODEL))
    yp_full, (p_k, p_v, p_conv, p_lconv, p_h) = trunk(jnp.concatenate([meta, x_prompt], axis=1), None, P)
    y_prompt = yp_full[:, N_META:]
    y_sample, (s_k, s_v, s_conv, s_lconv, s_h) = trunk(
        x_sample, (cache_attn_k, cache_attn_v, state_conv, state_lru_conv, state_lru_h), P)
    return (y_prompt, y_sample, p_k, p_v, p_conv, p_lconv, p_h, s_k, s_v, s_conv, s_lconv, s_h)
```

```python
import functools
import math

import jax
import jax.numpy as jnp
from jax import lax
from jax.experimental import pallas as pl
from jax.experimental.pallas import tpu as pltpu

F32 = jnp.float32
BF16 = jnp.bfloat16

D = 2048
B = 2
SEQ = 4096
DEPTH = 2
DB = 128
T = 8
PAST = 8192
N_META = 16
NH = 16
NKV = 4
G = NH // NKV
HD = 64
AW = NH * HD
KVW = NKV * HD
WIN = 128
BLK = 128
CW = D // 2
CK = 31
LW = D // 2
LCK = 4
LRU_C = 8.0
DFF = 5632
NE = 8
EPS = 1e-6
NEG = -1e30
IN_W = AW + 2 * KVW + 2 * CW + 2 * LW + 3 * D

PAD = (-N_META) % BLK
PB = PAD + N_META + SEQ
NB = PB // BLK
MP = B * PB
MS = DB * T
M = MP + MS

ZC = 512
NZC = IN_W // ZC
Z_Q, Z_A, Z_GATE, Z_LX, Z_LG, Z_GA, Z_K, Z_V = 0, 1024, 2048, 3072, 4096, 5120, 11264, 11520

SLOPES = tuple(2.0 ** (-8.0 * (h + 1.0) / NH) for h in range(NH))

TM_BIG = 1184
TM_MID = 592
TM_TOK = 256
SB = 8
MOE_TM = 256
MOE_TILES = (2 * M) // MOE_TM + NE
MOE_ROWS = MOE_TILES * MOE_TM
UP_TN = 1408
DOWN_TN = 1024

VMEM_LIMIT = 56 * 1024 * 1024


def _cp(*sem):
    return pltpu.CompilerParams(dimension_semantics=sem, vmem_limit_bytes=VMEM_LIMIT)


def _rms(x, g):
    return x * lax.rsqrt(jnp.mean(x * x, axis=-1, keepdims=True) + EPS) * g


def _sigmoid(x):
    return 1.0 / (1.0 + jnp.exp(-x))


def _silu(x):
    return x * _sigmoid(x)


def _norm_kernel(x_ref, g_ref, o_ref):
    o_ref[...] = _rms(x_ref[...], g_ref[...]).astype(o_ref.dtype)


def rmsnorm_cast(x, g):
    tm = TM_MID
    return pl.pallas_call(
        _norm_kernel,
        grid=(M // tm,),
        in_specs=[pl.BlockSpec((tm, D), lambda i: (i, 0)), pl.BlockSpec((1, D), lambda i: (0, 0))],
        out_specs=pl.BlockSpec((tm, D), lambda i: (i, 0)),
        out_shape=jax.ShapeDtypeStruct((M, D), BF16),
        compiler_params=_cp("parallel"),
        name="rmsnorm_cast",
    )(x, g.reshape(1, D))


def _inproj_kernel(x_ref, w_ref, o_ref):
    o_ref[...] = jnp.dot(x_ref[...], w_ref[...], preferred_element_type=F32).astype(o_ref.dtype)


def _z_src_block(j):
    return jnp.where(j < 2, j, jnp.where(j < NZC - 1, j + 1, 2))


def inproj(xn, w):
    tm = TM_BIG
    return pl.pallas_call(
        _inproj_kernel,
        grid=(M // tm, NZC),
        in_specs=[pl.BlockSpec((tm, D), lambda i, j: (i, 0)),
                  pl.BlockSpec((D, ZC), lambda i, j: (0, _z_src_block(j)))],
        out_specs=pl.BlockSpec((tm, ZC), lambda i, j: (i, j)),
        out_shape=jax.ShapeDtypeStruct((M, IN_W), BF16),
        compiler_params=_cp("parallel", "arbitrary"),
        name="inproj",
    )(xn, w)


def _softmax_sink(s, sink):
    m = jnp.maximum(jnp.max(s, axis=-1, keepdims=True), sink)
    e = jnp.exp(s - m)
    l = jnp.sum(e, axis=-1, keepdims=True) + jnp.exp(sink - m)
    return e / l


def _attn_prompt_kernel(sink_ref, q_ref, kp_ref, kc_ref, vp_ref, vc_ref, o_ref):
    n = pl.program_id(1)
    q = q_ref[...]
    k = jnp.concatenate([kp_ref[...], kc_ref[...]], axis=0)
    v = jnp.concatenate([vp_ref[...], vc_ref[...]], axis=0)
    r = lax.broadcasted_iota(jnp.int32, (BLK, 2 * BLK), 0)
    c = lax.broadcasted_iota(jnp.int32, (BLK, 2 * BLK), 1)
    dist = BLK + r - c
    kpos = (n - 1) * BLK - PAD + c
    allowed = (dist >= 0) & (dist <= WIN) & (kpos >= 0)
    distf = dist.astype(F32)
    for h in range(NKV):
        kh = k[:, h * HD:(h + 1) * HD]
        vh = v[:, h * HD:(h + 1) * HD]
        qh = jnp.concatenate([q[:, (h * G + g) * HD:(h * G + g + 1) * HD] for g in range(G)], axis=0)
        s = lax.dot_general(qh, kh, (((1,), (1,)), ((), ())), preferred_element_type=F32)
        ps = []
        for g in range(G):
            hq = h * G + g
            sg = s[g * BLK:(g + 1) * BLK] * (HD ** -0.5) - SLOPES[hq] * distf
            sg = jnp.where(allowed, sg, NEG)
            ps.append(_softmax_sink(sg, sink_ref[hq]).astype(BF16))
        oh = jnp.dot(jnp.concatenate(ps, axis=0), vh, preferred_element_type=F32)
        for g in range(G):
            hq = h * G + g
            o_ref[:, hq * HD:(hq + 1) * HD] = oh[g * BLK:(g + 1) * BLK].astype(o_ref.dtype)


def attn_prompt(z, sinks):
    kb, vb = Z_K // KVW, Z_V // KVW

    def cur(col):
        return lambda b, n: (b * NB + n, col)

    def prev(col):
        return lambda b, n: (b * NB + jnp.maximum(n - 1, 0), col)

    return pl.pallas_call(
        _attn_prompt_kernel,
        grid=(B, NB),
        in_specs=[pl.BlockSpec(memory_space=pltpu.SMEM),
                  pl.BlockSpec((BLK, AW), cur(0)),
                  pl.BlockSpec((BLK, KVW), prev(kb)), pl.BlockSpec((BLK, KVW), cur(kb)),
                  pl.BlockSpec((BLK, KVW), prev(vb)), pl.BlockSpec((BLK, KVW), cur(vb))],
        out_specs=pl.BlockSpec((BLK, AW), cur(0)),
        out_shape=jax.ShapeDtypeStruct((MP, AW), BF16),
        compiler_params=_cp("parallel", "arbitrary"),
        name="attn_prompt",
    )(sinks, z, z, z, z, z)


def _attn_sample_kernel(sink_ref, q_ref, kn_ref, vn_ref, ck_ref, cv_ref, nk_in_ref, nv_in_ref,
                        o_ref, nk_ref, nv_ref):
    del nk_in_ref, nv_in_ref
    q = q_ref[...].astype(F32).reshape(SB, T, AW)
    kn = kn_ref[...].astype(F32).reshape(SB, T, KVW)
    vn = vn_ref[...].astype(F32).reshape(SB, T, KVW)
    ck = ck_ref[...]
    cv = cv_ref[...]
    wb = WIN
    nk_ref[:, 0:wb - T, :] = ck[:, T:wb, :]
    nk_ref[:, wb - T:wb, :] = kn
    nv_ref[:, 0:wb - T, :] = cv[:, T:wb, :]
    nv_ref[:, wb - T:wb, :] = vn
    zpad = jnp.zeros((SB, wb - T, KVW), F32)
    k = jnp.concatenate([ck, kn, zpad], axis=1).astype(BF16)
    v = jnp.concatenate([cv, vn, zpad], axis=1).astype(BF16)
    nkeys = 2 * wb
    r = lax.broadcasted_iota(jnp.int32, (G * T, nkeys), 0)
    c = lax.broadcasted_iota(jnp.int32, (G * T, nkeys), 1)
    dist = wb + (r % T) - c
    allowed = (dist >= 0) & (dist <= WIN)
    distf = dist.astype(F32)
    gidx = r // T
    for h in range(NKV):
        slope = jnp.zeros((G * T, nkeys), F32)
        sink = jnp.zeros((G * T, 1), F32)
        for g in range(G):
            slope = jnp.where(gidx == g, SLOPES[h * G + g], slope)
            sink = jnp.where(gidx[:, 0:1] == g, sink_ref[h * G + g], sink)
        kh = k[:, :, h * HD:(h + 1) * HD]
        vh = v[:, :, h * HD:(h + 1) * HD]
        qh = jnp.concatenate([q[:, :, (h * G + g) * HD:(h * G + g + 1) * HD] for g in range(G)],
                             axis=1).astype(BF16)
        s = jnp.einsum("bqd,bkd->bqk", qh, kh, preferred_element_type=F32)
        s = s * (HD ** -0.5) - (slope * distf)[None]
        s = jnp.where(allowed[None], s, NEG)
        p = _softmax_sink(s, sink[None]).astype(BF16)
        oh = jnp.einsum("bqk,bkd->bqd", p, vh, preferred_element_type=F32)
        for g in range(G):
            hq = h * G + g
            piece = oh[:, g * T:(g + 1) * T, :].reshape(SB * T, HD)
            o_ref[:, hq * HD:(hq + 1) * HD] = piece.astype(o_ref.dtype)


def attn_sample(z, sinks, cache_k, cache_v, nk, nv, layer):
    rows = SB * T
    rb0 = MP // rows
    kb, vb = Z_K // KVW, Z_V // KVW
    cb0 = layer * (DB // SB)
    cspec = pl.BlockSpec((SB, WIN, KVW), lambda i: (cb0 + i, 0, 0))
    anyspec = pl.BlockSpec(memory_space=pl.ANY)
    return pl.pallas_call(
        _attn_sample_kernel,
        grid=(DB // SB,),
        in_specs=[pl.BlockSpec(memory_space=pltpu.SMEM),
                  pl.BlockSpec((rows, AW), lambda i: (rb0 + i, 0)),
                  pl.BlockSpec((rows, KVW), lambda i: (rb0 + i, kb)),
                  pl.BlockSpec((rows, KVW), lambda i: (rb0 + i, vb)),
                  cspec, cspec, anyspec, anyspec],
        out_specs=[pl.BlockSpec((rows, AW), lambda i: (i, 0)), cspec, cspec],
        out_shape=[jax.ShapeDtypeStruct((MS, AW), BF16),
                   jax.ShapeDtypeStruct((DEPTH * DB, WIN, KVW), F32),
                   jax.ShapeDtypeStruct((DEPTH * DB, WIN, KVW), F32)],
        input_output_aliases={6: 1, 7: 2},
        compiler_params=_cp("parallel"),
        name="attn_sample",
    )(sinks, z, z, z, cache_k, cache_v, nk, nv)


def _ln_swish(y, g, b):
    mu = jnp.mean(y, axis=-1, keepdims=True)
    yc = y - mu
    var = jnp.mean(yc * yc, axis=-1, keepdims=True)
    yn = yc * lax.rsqrt(var + EPS) * g + b
    return _silu(yn)


CONV_HIST = 32


def _conv_prompt_kernel(a_ref, g_ref, w_ref, b_ref, lg_ref, lb_ref, c_ref, st_ref, ext_ref, y_ref):
    n = pl.program_id(1)

    @pl.when(n == 0)
    def _():
        ext_ref[0:CONV_HIST, :] = jnp.zeros((CONV_HIST, CW), F32)

    @pl.when(n > 0)
    def _():
        ext_ref[0:CONV_HIST, :] = ext_ref[BLK:BLK + CONV_HIST, :]

    u = a_ref[...].astype(F32) * _sigmoid(g_ref[...].astype(F32))
    row = lax.broadcasted_iota(jnp.int32, (BLK, 1), 0)
    u = jnp.where((n > 0) | (row >= PAD), u, 0.0)
    ext_ref[CONV_HIST:CONV_HIST + BLK, :] = u
    off = CONV_HIST - (CK - 1)
    cc = 256
    for c0 in range(0, CW, cc):
        acc = jnp.zeros((BLK, cc), F32) + b_ref[:, c0:c0 + cc]
        for j in range(CK):
            acc = acc + w_ref[j:j + 1, c0:c0 + cc] * ext_ref[off + j:off + j + BLK, c0:c0 + cc]
        y_ref[:, c0:c0 + cc] = acc
    c_ref[...] = _ln_swish(y_ref[...], lg_ref[...], lb_ref[...]).astype(c_ref.dtype)

    @pl.when(n == NB - 1)
    def _():
        st_ref[0] = ext_ref[CONV_HIST + BLK - (CK - 1):CONV_HIST + BLK, :]


def conv_prompt(z, w, b, lg, lb):
    vec = pl.BlockSpec((1, CW), lambda bb, n: (0, 0))
    return pl.pallas_call(
        _conv_prompt_kernel,
        grid=(B, NB),
        in_specs=[pl.BlockSpec((BLK, CW), lambda bb, n: (bb * NB + n, Z_A // CW)),
                  pl.BlockSpec((BLK, CW), lambda bb, n: (bb * NB + n, Z_GATE // CW)),
                  pl.BlockSpec((CK, CW), lambda bb, n: (0, 0)), vec, vec, vec],
        out_specs=[pl.BlockSpec((BLK, CW), lambda bb, n: (bb * NB + n, 0)),
                   pl.BlockSpec((1, CK - 1, CW), lambda bb, n: (bb, 0, 0))],
        out_shape=[jax.ShapeDtypeStruct((MP, CW), BF16), jax.ShapeDtypeStruct((B, CK - 1, CW), F32)],
        scratch_shapes=[pltpu.VMEM((CONV_HIST + BLK, CW), F32), pltpu.VMEM((BLK, CW), F32)],
        compiler_params=_cp("parallel", "arbitrary"),
        name="conv_prompt",
    )(z, z, w, b.reshape(1, CW), lg.reshape(1, CW), lb.reshape(1, CW))


def _conv_sample_kernel(a_ref, g_ref, st_ref, w_ref, b_ref, lg_ref, lb_ref, nst_in_ref,
                        c_ref, nst_ref, ext_ref, y_ref):
    del nst_in_ref
    u = a_ref[...].astype(F32) * _sigmoid(g_ref[...].astype(F32))
    hist = CK - 1
    base = CONV_HIST - hist
    for s in range(SB):
        ext_ref[s, base:CONV_HIST, :] = st_ref[s]
        ext_ref[s, CONV_HIST:CONV_HIST + T, :] = u[s * T:(s + 1) * T]
    for s in range(SB):
        acc = jnp.zeros((T, CW), F32) + b_ref[...]
        for j in range(CK):
            acc = acc + w_ref[j:j + 1, :] * ext_ref[s, base + j:base + j + T, :]
        y_ref[s * T:(s + 1) * T, :] = acc
        nst_ref[s] = ext_ref[s, base + T:base + T + hist, :]
    c_ref[...] = _ln_swish(y_ref[...], lg_ref[...], lb_ref[...]).astype(c_ref.dtype)


def conv_sample(z, state, w, b, lg, lb, nst, layer):
    rows = SB * T
    rb0 = MP // rows
    sb0 = layer * (DB // SB)
    vec = pl.BlockSpec((1, CW), lambda i: (0, 0))
    sspec = pl.BlockSpec((SB, CK - 1, CW), lambda i: (sb0 + i, 0, 0))
    anyspec = pl.BlockSpec(memory_space=pl.ANY)
    return pl.pallas_call(
        _conv_sample_kernel,
        grid=(DB // SB,),
        in_specs=[pl.BlockSpec((rows, CW), lambda i: (rb0 + i, Z_A // CW)),
                  pl.BlockSpec((rows, CW), lambda i: (rb0 + i, Z_GATE // CW)),
                  sspec, pl.BlockSpec((CK, CW), lambda i: (0, 0)), vec, vec, vec, anyspec],
        out_specs=[pl.BlockSpec((rows, CW), lambda i: (i, 0)), sspec],
        out_shape=[jax.ShapeDtypeStruct((MS, CW), BF16), jax.ShapeDtypeStruct((DEPTH * DB, CK - 1, CW), F32)],
        scratch_shapes=[pltpu.VMEM((SB, CONV_HIST + T, CW), F32), pltpu.VMEM((rows, CW), F32)],
        input_output_aliases={7: 1},
        compiler_params=_cp("parallel"),
        name="conv_sample",
    )(z, z, state, w, b.reshape(1, CW), lg.reshape(1, CW), lb.reshape(1, CW), nst)


LRU_HIST = 8
GATE_CH = 256


def _gelu_tanh(x):
    return x * (0.5 * (1.0 + jnp.tanh(math.sqrt(2.0 / math.pi) * (x + 0.044715 * (x * x * x)))))


def _softplus(x):
    return jnp.maximum(x, 0.0) + jnp.log1p(jnp.exp(-jnp.abs(x)))


def _expm1(x):
    return jnp.tanh(0.5 * x) * (jnp.exp(x) + 1.0)


def _lru_gates(xc, wr_ref, br_ref, wi_ref, bi_ref, lam_ref, valid, a_ref, bx_ref):
    sp = _softplus(-lam_ref[...])
    for k in range(LW // GATE_CH):
        sl = slice(k * GATE_CH, (k + 1) * GATE_CH)
        xk = xc[:, sl]
        xkb = xk.astype(BF16)
        r = _sigmoid(jnp.dot(xkb, wr_ref[k], preferred_element_type=F32) + br_ref[:, sl])
        i = _sigmoid(jnp.dot(xkb, wi_ref[k], preferred_element_type=F32) + bi_ref[:, sl])
        log_a = (-LRU_C) * r * sp[:, sl]
        a = jnp.exp(log_a)
        bx = jnp.sqrt(-_expm1(2.0 * log_a)) * (i * xk)
        if valid is not None:
            bx = jnp.where(valid, bx, 0.0)
        a_ref[:, sl] = a
        bx_ref[:, sl] = bx


def _lru_prompt_kernel(x_ref, g_ref, cw_ref, cb_ref, wr_ref, br_ref, wi_ref, bi_ref, lam_ref,
                       rr_ref, cst_ref, hst_ref, ext_ref, a_ref, bx_ref, h_ref, carry_ref):
    n = pl.program_id(1)

    @pl.when(n == 0)
    def _():
        ext_ref[0:LRU_HIST, :] = jnp.zeros((LRU_HIST, LW), F32)
        carry_ref[...] = jnp.zeros((1, LW), F32)

    @pl.when(n > 0)
    def _():
        ext_ref[0:LRU_HIST, :] = ext_ref[BLK:BLK + LRU_HIST, :]

    row = lax.broadcasted_iota(jnp.int32, (BLK, 1), 0)
    valid = (n > 0) | (row >= PAD)
    ext_ref[LRU_HIST:LRU_HIST + BLK, :] = jnp.where(valid, x_ref[...].astype(F32), 0.0)
    off = LRU_HIST - (LCK - 1)
    xc = jnp.zeros((BLK, LW), F32) + cb_ref[...]
    for j in range(LCK):
        xc = xc + cw_ref[j:j + 1, :] * ext_ref[off + j:off + j + BLK, :]
    _lru_gates(xc, wr_ref, br_ref, wi_ref, bi_ref, lam_ref, valid, a_ref, bx_ref)

    def step(t, h):
        h = a_ref[pl.ds(t, 1), :] * h + bx_ref[pl.ds(t, 1), :]
        h_ref[pl.ds(t, 1), :] = h
        return h

    h = lax.fori_loop(0, BLK, step, carry_ref[...], unroll=8)
    carry_ref[...] = h
    rr_ref[...] = (h_ref[...] * _gelu_tanh(g_ref[...].astype(F32))).astype(rr_ref.dtype)

    @pl.when(n == NB - 1)
    def _():
        cst_ref[0] = ext_ref[LRU_HIST + BLK - (LCK - 1):LRU_HIST + BLK, :]
        hst_ref[0] = h


def lru_prompt(z, cw, cb, wr, br, wi, bi, lam):
    vec = pl.BlockSpec((1, LW), lambda bb, n: (0, 0))
    wspec = pl.BlockSpec((LW // GATE_CH, GATE_CH, GATE_CH), lambda bb, n: (0, 0, 0))
    return pl.pallas_call(
        _lru_prompt_kernel,
        grid=(B, NB),
        in_specs=[pl.BlockSpec((BLK, LW), lambda bb, n: (bb * NB + n, Z_LX // LW)),
                  pl.BlockSpec((BLK, LW), lambda bb, n: (bb * NB + n, Z_LG // LW)),
                  pl.BlockSpec((LCK, LW), lambda bb, n: (0, 0)), vec, wspec, vec, wspec, vec, vec],
        out_specs=[pl.BlockSpec((BLK, LW), lambda bb, n: (bb * NB + n, 0)),
                   pl.BlockSpec((1, LCK - 1, LW), lambda bb, n: (bb, 0, 0)),
                   pl.BlockSpec((1, 1, LW), lambda bb, n: (bb, 0, 0))],
        out_shape=[jax.ShapeDtypeStruct((MP, LW), BF16), jax.ShapeDtypeStruct((B, LCK - 1, LW), F32),
                   jax.ShapeDtypeStruct((B, 1, LW), F32)],
        scratch_shapes=[pltpu.VMEM((LRU_HIST + BLK, LW), F32), pltpu.VMEM((BLK, LW), F32),
                        pltpu.VMEM((BLK, LW), F32), pltpu.VMEM((BLK, LW), F32), pltpu.VMEM((1, LW), F32)],
        compiler_params=_cp("parallel", "arbitrary"),
        name="lru_prompt",
    )(z, z, cw, cb.reshape(1, LW), wr, br.reshape(1, LW), wi, bi.reshape(1, LW), lam.reshape(1, LW))


def _lru_sample_kernel(x_ref, g_ref, cst_ref, h0_ref, cw_ref, cb_ref, wr_ref, br_ref, wi_ref, bi_ref, lam_ref,
                       ncst_in_ref, nh_in_ref, rr_ref, ncst_ref, nh_ref,
                       ext_ref, xc_ref, a_ref, bx_ref, h_ref):
    del ncst_in_ref, nh_in_ref
    hist = LCK - 1
    base = LRU_HIST - hist
    x = x_ref[...].astype(F32)
    for s in range(SB):
        ext_ref[s, base:LRU_HIST, :] = cst_ref[s]
        ext_ref[s, LRU_HIST:LRU_HIST + T, :] = x[s * T:(s + 1) * T]
    for s in range(SB):
        acc = jnp.zeros((T, LW), F32) + cb_ref[...]
        for j in range(LCK):
            acc = acc + cw_ref[j:j + 1, :] * ext_ref[s, base + j:base + j + T, :]
        xc_ref[s * T:(s + 1) * T, :] = acc
        ncst_ref[s] = ext_ref[s, base + T:base + T + hist, :]
    _lru_gates(xc_ref[...], wr_ref, br_ref, wi_ref, bi_ref, lam_ref, None, a_ref, bx_ref)
    for s in range(SB):
        h = h0_ref[s:s + 1, :]
        for t in range(T):
            rw = s * T + t
            h = a_ref[rw:rw + 1, :] * h + bx_ref[rw:rw + 1, :]
            h_ref[rw:rw + 1, :] = h
        nh_ref[s:s + 1, :] = h
    rr_ref[...] = (h_ref[...] * _gelu_tanh(g_ref[...].astype(F32))).astype(rr_ref.dtype)


def lru_sample(z, cstate, hstate, cw, cb, wr, br, wi, bi, lam, ncst, nh, layer):
    rows = SB * T
    rb0 = MP // rows
    sb0 = layer * (DB // SB)
    vec = pl.BlockSpec((1, LW), lambda i: (0, 0))
    wspec = pl.BlockSpec((LW // GATE_CH, GATE_CH, GATE_CH), lambda i: (0, 0, 0))
    cspec = pl.BlockSpec((SB, LCK - 1, LW), lambda i: (sb0 + i, 0, 0))
    hspec = pl.BlockSpec((SB, LW), lambda i: (sb0 + i, 0))
    anyspec = pl.BlockSpec(memory_space=pl.ANY)
    return pl.pallas_call(
        _lru_sample_kernel,
        grid=(DB // SB,),
        in_specs=[pl.BlockSpec((rows, LW), lambda i: (rb0 + i, Z_LX // LW)),
                  pl.BlockSpec((rows, LW), lambda i: (rb0 + i, Z_LG // LW)),
                  cspec, hspec, pl.BlockSpec((LCK, LW), lambda i: (0, 0)), vec, wspec, vec, wspec, vec, vec,
                  anyspec, anyspec],
        out_specs=[pl.BlockSpec((rows, LW), lambda i: (i, 0)), cspec, hspec],
        out_shape=[jax.ShapeDtypeStruct((MS, LW), BF16), jax.ShapeDtypeStruct((DEPTH * DB, LCK - 1, LW), F32),
                   jax.ShapeDtypeStruct((DEPTH * DB, LW), F32)],
        scratch_shapes=[pltpu.VMEM((SB, LRU_HIST + T, LW), F32)] + [pltpu.VMEM((rows, LW), F32)] * 4,
        input_output_aliases={11: 1, 12: 2},
        compiler_params=_cp("parallel"),
        name="lru_sample",
    )(z, z, cstate, hstate, cw, cb.reshape(1, LW), wr, br.reshape(1, LW), wi, bi.reshape(1, LW),
      lam.reshape(1, LW), ncst, nh)


ROUTE_W = 128


def _mix_body(branch_refs, gates, x_ref, wa_ref, wc_ref, wl_ref, wm_ref, g_ref, x1_ref):
    is_prompt = pl.program_id(0) < MP // TM_TOK
    o, c, r = (jnp.where(is_prompt, p_ref[...], s_ref[...]) for p_ref, s_ref in branch_refs)
    half = D // 2
    parts = []
    for hh in range(2):
        sl = slice(hh * half, (hh + 1) * half)
        m = _sigmoid(gates[0][hh][...].astype(F32)) * jnp.dot(o, wa_ref[:, sl], preferred_element_type=F32)
        m = m + _sigmoid(gates[1][hh][...].astype(F32)) * jnp.dot(c, wc_ref[:, sl], preferred_element_type=F32)
        m = m + _sigmoid(gates[2][hh][...].astype(F32)) * jnp.dot(r, wl_ref[:, sl], preferred_element_type=F32)
        parts.append(m.astype(BF16))
    merged = jnp.concatenate(parts, axis=1)
    x1 = x_ref[...] + jnp.dot(merged, wm_ref[...], preferred_element_type=F32)
    x1_ref[...] = x1
    return _rms(x1, g_ref[...])


def _mix_kernel(op_ref, os_ref, cp_ref, cs_ref, rp_ref, rs_ref, ga0, ga1, gb0, gb1, gc0, gc1, x_ref,
                wa_ref, wc_ref, wl_ref, wm_ref, g_ref, x1_ref, hn_ref):
    hn = _mix_body(((op_ref, os_ref), (cp_ref, cs_ref), (rp_ref, rs_ref)), ((ga0, ga1), (gb0, gb1), (gc0, gc1)),
                   x_ref, wa_ref, wc_ref, wl_ref, wm_ref, g_ref, x1_ref)
    hn_ref[...] = hn.astype(hn_ref.dtype)


def _mix_router_kernel(op_ref, os_ref, cp_ref, cs_ref, rp_ref, rs_ref, ga0, ga1, gb0, gb1, gc0, gc1, x_ref,
                       wa_ref, wc_ref, wl_ref, wm_ref, g_ref, rw_ref, rb_ref, x1_ref, route_ref):
    hn = _mix_body(((op_ref, os_ref), (cp_ref, cs_ref), (rp_ref, rs_ref)), ((ga0, ga1), (gb0, gb1), (gc0, gc1)),
                   x_ref, wa_ref, wc_ref, wl_ref, wm_ref, g_ref, x1_ref)
    logits = jnp.dot(hn, rw_ref[...], preferred_element_type=F32, precision=lax.Precision.HIGHEST) + rb_ref[...]
    lane = lax.broadcasted_iota(jnp.int32, logits.shape, 1)
    ninf = -jnp.inf
    l1 = jnp.where(lane < NE, logits, ninf)
    m1 = jnp.max(l1, axis=-1, keepdims=True)
    i1 = jnp.min(jnp.where(l1 == m1, lane, ROUTE_W), axis=-1, keepdims=True)
    l2 = jnp.where(lane == i1, ninf, l1)
    m2 = jnp.max(l2, axis=-1, keepdims=True)
    i2 = jnp.min(jnp.where(l2 == m2, lane, ROUTE_W), axis=-1, keepdims=True)
    e2 = jnp.exp(m2 - m1)
    den = 1.0 + e2
    out = jnp.where(lane == 0, i1.astype(F32), 0.0)
    out = jnp.where(lane == 1, i2.astype(F32), out)
    out = jnp.where(lane == 2, 1.0 / den, out)
    out = jnp.where(lane == 3, e2 / den, out)
    route_ref[...] = out


def mix(o, c, rr, z, x, wa, wc, wl, wm, g, router=None):
    tm = TM_TOK
    half = D // 2
    npt = MP // tm
    row = lambda col: (lambda i: (i, col))
    const = lambda shape: pl.BlockSpec(shape, lambda i: (0,) * len(shape), pipeline_mode=pl.Buffered(1))
    gate_specs = [pl.BlockSpec((tm, half), row(Z_GA // half + k)) for k in range(6)]
    pair = lambda w: [pl.BlockSpec((tm, w), lambda i: (jnp.minimum(i, npt - 1), 0)),
                      pl.BlockSpec((tm, w), lambda i: (jnp.maximum(i - npt, 0), 0))]
    in_specs = (pair(AW) + pair(CW) + pair(LW) + gate_specs
                + [pl.BlockSpec((tm, D), row(0)), const((AW, D)), const((CW, D)), const((LW, D)), const((D, D)),
                   const((1, D))])
    args = [*o, *c, *rr, z, z, z, z, z, z, x, wa, wc, wl, wm, g.reshape(1, D)]
    if router is None:
        kern = _mix_kernel
        out_specs = [pl.BlockSpec((tm, D), row(0)), pl.BlockSpec((tm, D), row(0))]
        out_shape = [jax.ShapeDtypeStruct((M, D), F32), jax.ShapeDtypeStruct((M, D), BF16)]
    else:
        kern = _mix_router_kernel
        rw, rb = router
        in_specs += [const((D, ROUTE_W)), const((1, ROUTE_W))]
        args += [rw, rb]
        out_specs = [pl.BlockSpec((tm, D), row(0)), pl.BlockSpec((tm, ROUTE_W), row(0))]
        out_shape = [jax.ShapeDtypeStruct((M, D), F32), jax.ShapeDtypeStruct((M, ROUTE_W), F32)]
    return pl.pallas_call(
        kern,
        grid=(M // tm,),
        in_specs=in_specs,
        out_specs=out_specs,
        out_shape=out_shape,
        compiler_params=_cp("parallel"),
        name="mix",
    )(*args)


def _ffn_up_kernel(x_ref, w1_ref, w3_ref, o_ref):
    x = x_ref[...]
    a = jnp.dot(x, w1_ref[...], preferred_element_type=F32)
    b = jnp.dot(x, w3_ref[...], preferred_element_type=F32)
    o_ref[...] = (_silu(a) * b).astype(o_ref.dtype)


def ffn_up(hn, w1, w3):
    tm, tn = TM_BIG, 512
    return pl.pallas_call(
        _ffn_up_kernel,
        grid=(M // tm, DFF // tn),
        in_specs=[pl.BlockSpec((tm, D), lambda i, j: (i, 0)),
                  pl.BlockSpec((D, tn), lambda i, j: (0, j)), pl.BlockSpec((D, tn), lambda i, j: (0, j))],
        out_specs=pl.BlockSpec((tm, tn), lambda i, j: (i, j)),
        out_shape=jax.ShapeDtypeStruct((M, DFF), BF16),
        compiler_params=_cp("parallel", "arbitrary"),
        name="ffn_up",
    )(hn, w1, w3)


def _ffn_down_kernel(h_ref, w_ref, x_ref, g_ref, x2_ref, xn_ref):
    x2 = x_ref[...] + jnp.dot(h_ref[...], w_ref[...], preferred_element_type=F32)
    x2_ref[...] = x2
    xn_ref[...] = _rms(x2, g_ref[...]).astype(xn_ref.dtype)


def ffn_down(h, w2, x1, g):
    tm = TM_TOK
    return pl.pallas_call(
        _ffn_down_kernel,
        grid=(M // tm,),
        in_specs=[pl.BlockSpec((tm, DFF), lambda i: (i, 0)),
                  pl.BlockSpec((DFF, D), lambda i: (0, 0), pipeline_mode=pl.Buffered(1)),
                  pl.BlockSpec((tm, D), lambda i: (i, 0)), pl.BlockSpec((1, D), lambda i: (0, 0))],
        out_specs=[pl.BlockSpec((tm, D), lambda i: (i, 0)), pl.BlockSpec((tm, D), lambda i: (i, 0))],
        out_shape=[jax.ShapeDtypeStruct((M, D), F32), jax.ShapeDtypeStruct((M, D), BF16)],
        compiler_params=_cp("parallel"),
        name="ffn_down",
    )(h, w2, x1, g.reshape(1, D))


def _row_copy(src, src_row, dst, dst_row, sem):
    return pltpu.make_async_copy(src.at[pl.ds(src_row, 1)], dst.at[pl.ds(dst_row, 1)], sem)


def _moe_scatter_kernel(s1_ref, s2_ref, zt_ref, x_ref, g_ref, xs_ref, buf_ref, zero_ref, sem):
    i = pl.program_id(0)
    tm = MOE_TM

    def tile_copy(e):
        return pltpu.make_async_copy(zero_ref, xs_ref.at[pl.ds(pl.multiple_of(zt_ref[e] * tm, tm), tm)], sem)

    @pl.when(i == 0)
    def _():
        zero_ref[...] = jnp.zeros(zero_ref.shape, F32)
        for e in range(2 * NE):
            tile_copy(e).start()
            tile_copy(e).wait()

    buf_ref[...] = _rms(x_ref[...], g_ref[...])

    def issue(r, carry):
        t = i * tm + r
        _row_copy(buf_ref, r, xs_ref, s1_ref[t], sem).start()
        _row_copy(buf_ref, r, xs_ref, s2_ref[t], sem).start()
        return carry

    lax.fori_loop(0, tm, issue, 0)

    def drain(r, carry):
        _row_copy(buf_ref, 0, xs_ref, 0, sem).wait()
        _row_copy(buf_ref, 0, xs_ref, 0, sem).wait()
        return carry

    lax.fori_loop(0, tm, drain, 0)


def moe_scatter(x1, g, slot1, slot2, zero_tiles):
    tm = MOE_TM
    return pl.pallas_call(
        _moe_scatter_kernel,
        grid_spec=pltpu.PrefetchScalarGridSpec(
            num_scalar_prefetch=3,
            grid=(M // tm,),
            in_specs=[pl.BlockSpec((tm, D), lambda i, *_: (i, 0)), pl.BlockSpec((1, D), lambda i, *_: (0, 0))],
            out_specs=pl.BlockSpec(memory_space=pl.ANY),
            scratch_shapes=[pltpu.VMEM((tm, D), F32), pltpu.VMEM((tm, D), F32), pltpu.SemaphoreType.DMA(())]),
        out_shape=jax.ShapeDtypeStruct((MOE_ROWS, D), F32),
        compiler_params=_cp("arbitrary"),
        name="moe_scatter",
    )(slot1, slot2, zero_tiles, x1, g.reshape(1, D))


def _moe_up_kernel(te_ref, nu_ref, x_ref, w1_ref, w3_ref, o_ref):
    used = pl.program_id(1) < nu_ref[0]

    @pl.when(used)
    def _():
        x = x_ref[...].astype(BF16)
        a = jnp.dot(x, w1_ref[0], preferred_element_type=F32)
        b = jnp.dot(x, w3_ref[0], preferred_element_type=F32)
        o_ref[...] = (_silu(a) * b).astype(o_ref.dtype)

    @pl.when(jnp.logical_not(used))
    def _():
        o_ref[...] = jnp.zeros(o_ref.shape, o_ref.dtype)


def _used(i, nu_ref):
    return jnp.minimum(i, nu_ref[0] - 1)


def moe_up(xs, w1, w3, tile_expert, n_used):
    tm, tn = MOE_TM, UP_TN
    wspec = pl.BlockSpec((1, D, tn), lambda j, i, te, nu: (te[_used(i, nu)], 0, j))
    return pl.pallas_call(
        _moe_up_kernel,
        grid_spec=pltpu.PrefetchScalarGridSpec(
            num_scalar_prefetch=2,
            grid=(DFF // tn, MOE_TILES),
            in_specs=[pl.BlockSpec((tm, D), lambda j, i, te, nu: (_used(i, nu), 0)), wspec, wspec],
            out_specs=pl.BlockSpec((tm, tn), lambda j, i, te, nu: (i, j))),
        out_shape=jax.ShapeDtypeStruct((MOE_ROWS, DFF), BF16),
        compiler_params=_cp("arbitrary", "arbitrary"),
        name="moe_up",
    )(tile_expert, n_used, xs, w1, w3)


def _moe_down_kernel(te_ref, nu_ref, h_ref, w_ref, o_ref):
    used = pl.program_id(1) < nu_ref[0]

    @pl.when(used)
    def _():
        o_ref[...] = jnp.dot(h_ref[...], w_ref[0], preferred_element_type=F32)

    @pl.when(jnp.logical_not(used))
    def _():
        o_ref[...] = jnp.zeros(o_ref.shape, o_ref.dtype)


def moe_down(hs, w2, tile_expert, n_used):
    tm, tn = MOE_TM, DOWN_TN
    return pl.pallas_call(
        _moe_down_kernel,
        grid_spec=pltpu.PrefetchScalarGridSpec(
            num_scalar_prefetch=2,
            grid=(D // tn, MOE_TILES),
            in_specs=[pl.BlockSpec((tm, DFF), lambda j, i, te, nu: (_used(i, nu), 0)),
                      pl.BlockSpec((1, DFF, tn), lambda j, i, te, nu: (te[_used(i, nu)], 0, j))],
            out_specs=pl.BlockSpec((tm, tn), lambda j, i, te, nu: (i, j))),
        out_shape=jax.ShapeDtypeStruct((MOE_ROWS, D), F32),
        compiler_params=_cp("arbitrary", "arbitrary"),
        name="moe_down",
    )(tile_expert, n_used, hs, w2)


def _moe_combine_kernel(s1_ref, s2_ref, x_ref, route_ref, g_ref, ys_ref, y_ref, ya_ref, yb_ref, sem):
    i = pl.program_id(0)
    tm = MOE_TM

    def issue(r, carry):
        t = i * tm + r
        _row_copy(ys_ref, s1_ref[t], ya_ref, r, sem).start()
        _row_copy(ys_ref, s2_ref[t], yb_ref, r, sem).start()
        return carry

    lax.fori_loop(0, tm, issue, 0)

    def drain(r, carry):
        _row_copy(ys_ref, 0, ya_ref, 0, sem).wait()
        _row_copy(ys_ref, 0, yb_ref, 0, sem).wait()
        return carry

    lax.fori_loop(0, tm, drain, 0)
    w1 = route_ref[:, 2:3]
    w2 = route_ref[:, 3:4]
    x2 = x_ref[...] + (w1 * ya_ref[...] + w2 * yb_ref[...])
    y_ref[...] = _rms(x2, g_ref[...])


def moe_combine(x1, route, g, ys, slot1, slot2):
    tm = MOE_TM
    return pl.pallas_call(
        _moe_combine_kernel,
        grid_spec=pltpu.PrefetchScalarGridSpec(
            num_scalar_prefetch=2,
            grid=(M // tm,),
            in_specs=[pl.BlockSpec((tm, D), lambda i, *_: (i, 0)), pl.BlockSpec((tm, ROUTE_W), lambda i, *_: (i, 0)),
                      pl.BlockSpec((1, D), lambda i, *_: (0, 0)), pl.BlockSpec(memory_space=pl.ANY)],
            out_specs=pl.BlockSpec((tm, D), lambda i, *_: (i, 0)),
            scratch_shapes=[pltpu.VMEM((tm, D), F32), pltpu.VMEM((tm, D), F32), pltpu.SemaphoreType.DMA(())]),
        out_shape=jax.ShapeDtypeStruct((M, D), F32),
        compiler_params=_cp("arbitrary"),
        name="moe_combine",
    )(slot1, slot2, x1, route, g.reshape(1, D), ys)


def _moe_plan(route):
    e1 = route[:, 0].astype(jnp.int32)
    e2 = route[:, 1].astype(jnp.int32)
    ids = jnp.arange(NE, dtype=jnp.int32)
    sel = (e1[:, None] == ids).astype(jnp.int32) + (e2[:, None] == ids).astype(jnp.int32)
    incl = jnp.cumsum(sel, axis=0)
    rank = incl - sel
    cnt = incl[-1]
    ntile = (cnt + MOE_TM - 1) // MOE_TM
    tile_end = jnp.cumsum(ntile)
    tile_off = tile_end - ntile
    row_off = tile_off * MOE_TM
    slot1 = jnp.take(row_off, e1) + jnp.take_along_axis(rank, e1[:, None], axis=1)[:, 0]
    slot2 = jnp.take(row_off, e2) + jnp.take_along_axis(rank, e2[:, None], axis=1)[:, 0]
    tiles = jnp.arange(MOE_TILES, dtype=jnp.int32)
    tile_expert = jnp.minimum(jnp.sum((tiles[:, None] >= tile_end[None, :]).astype(jnp.int32), axis=1), NE - 1)
    n_used = tile_end[-1:].astype(jnp.int32)
    tail = jnp.minimum(tile_end[-1] + ids, MOE_TILES - 1)
    zero_tiles = jnp.concatenate([jnp.maximum(tile_end - 1, 0), tail]).astype(jnp.int32)
    return slot1.astype(jnp.int32), slot2.astype(jnp.int32), tile_expert.astype(jnp.int32), n_used, zero_tiles


def _blockdiag(w):
    per = GATE_CH // (LW // 16)
    w4 = w.reshape(LW // GATE_CH, per, 64, 64)
    eye = jnp.eye(per, dtype=w.dtype)
    return jnp.einsum("cnde,nm->cndme", w4, eye).reshape(LW // GATE_CH, GATE_CH, GATE_CH).astype(BF16)


def kernel(x_prompt, x_sample, cache_attn_k, cache_attn_v, state_conv, state_lru_conv, state_lru_h, meta_tokens,
           norm_mix, norm_ffn, norm_final, w_in, attn_sinks, w_attn_out, conv_dw_w, conv_dw_b, conv_ln_g, conv_ln_b,
           w_conv_out, lru_conv_w, lru_conv_b, lru_w_r, lru_b_r, lru_w_i, lru_b_i, lru_lambda, w_lru_out, w_mix_out,
           ffn_w1, ffn_w3, ffn_w2, moe_router_w, moe_router_b, moe_w1, moe_w3, moe_w2):
    front = jnp.zeros((B, PAD, D), F32)
    meta = jnp.broadcast_to(meta_tokens[None], (B, N_META, D))
    x = jnp.concatenate([jnp.concatenate([front, meta, x_prompt], axis=1).reshape(MP, D),
                         x_sample.reshape(MS, D)], axis=0)

    ck = cache_attn_k.reshape(DEPTH * DB, WIN, KVW)
    cv = cache_attn_v.reshape(DEPTH * DB, WIN, KVW)
    sc = state_conv.reshape(DEPTH * DB, CK - 1, CW)
    slc = state_lru_conv.reshape(DEPTH * DB, LCK - 1, LW)
    slh = state_lru_h.reshape(DEPTH * DB, LW)

    s_k = jnp.zeros((DEPTH * DB, WIN, KVW), F32)
    s_v = jnp.zeros((DEPTH * DB, WIN, KVW), F32)
    s_conv = jnp.zeros((DEPTH * DB, CK - 1, CW), F32)
    s_lconv = jnp.zeros((DEPTH * DB, LCK - 1, LW), F32)
    s_h = jnp.zeros((DEPTH * DB, LW), F32)
    p_k, p_v, p_conv, p_lconv, p_h = [], [], [], [], []

    xn = rmsnorm_cast(x, norm_mix[0])
    y = None
    for l in range(DEPTH):
        z = inproj(xn, w_in[l].astype(BF16))
        sinks = attn_sinks[l]

        o_p = attn_prompt(z, sinks)
        o_s, s_k, s_v = attn_sample(z, sinks, ck, cv, s_k, s_v, l)
        o = (o_p, o_s)
        kv_last = z[:MP].reshape(B, PB, IN_W)[:, PB - WIN:, Z_K:Z_K + 2 * KVW].astype(F32)
        p_k.append(kv_last[:, :, :KVW].reshape(B, WIN, NKV, HD))
        p_v.append(kv_last[:, :, KVW:].reshape(B, WIN, NKV, HD))

        c_p, pc = conv_prompt(z, conv_dw_w[l], conv_dw_b[l], conv_ln_g[l], conv_ln_b[l])
        c_s, s_conv = conv_sample(z, sc, conv_dw_w[l], conv_dw_b[l], conv_ln_g[l], conv_ln_b[l], s_conv, l)
        c = (c_p, c_s)
        p_conv.append(pc)

        wr = _blockdiag(lru_w_r[l])
        wi = _blockdiag(lru_w_i[l])
        lru_args = (lru_conv_w[l], lru_conv_b[l], wr, lru_b_r[l], wi, lru_b_i[l], lru_lambda[l])
        rr_p, plc, ph = lru_prompt(z, *lru_args)
        rr_s, s_lconv, s_h = lru_sample(z, slc, slh, *lru_args, s_lconv, s_h, l)
        rr = (rr_p, rr_s)
        p_lconv.append(plc)
        p_h.append(ph.reshape(B, LW))

        wa = w_attn_out[l].astype(BF16)
        wc = w_conv_out[l].astype(BF16)
        wl = w_lru_out[l].astype(BF16)
        wm = w_mix_out[l].astype(BF16)
        if l % 2 == 0:
            j = l // 2
            x1, hn = mix(o, c, rr, z, x, wa, wc, wl, wm, norm_ffn[l])
            h = ffn_up(hn, ffn_w1[j].astype(BF16), ffn_w3[j].astype(BF16))
            g_next = norm_mix[l + 1] if l + 1 < DEPTH else norm_final
            x, xn = ffn_down(h, ffn_w2[j].astype(BF16), x1, g_next)
            if l + 1 == DEPTH:
                y = xn
        else:
            j = l // 2
            rw = jnp.pad(moe_router_w[j], ((0, 0), (0, ROUTE_W - NE)))
            rb = jnp.pad(moe_router_b[j].reshape(1, NE), ((0, 0), (0, ROUTE_W - NE)))
            x1, route = mix(o, c, rr, z, x, wa, wc, wl, wm, norm_ffn[l], router=(rw, rb))
            slot1, slot2, tile_expert, n_used, zero_tiles = _moe_plan(route)
            xs = moe_scatter(x1, norm_ffn[l], slot1, slot2, zero_tiles)
            hs = moe_up(xs, moe_w1[j].astype(BF16), moe_w3[j].astype(BF16), tile_expert, n_used)
            ys = moe_down(hs, moe_w2[j].astype(BF16), tile_expert, n_used)
            g_next = norm_mix[l + 1] if l + 1 < DEPTH else norm_final
            y = moe_combine(x1, route, g_next, ys, slot1, slot2)
            x = None

    y_prompt = y[:MP].reshape(B, PB, D)[:, PAD + N_META:]
    y_sample = y[MP:].reshape(DB, T, D)
    st4 = lambda a: a.reshape(DEPTH, DB, WIN, NKV, HD)
    return (y_prompt, y_sample,
            jnp.stack(p_k), jnp.stack(p_v), jnp.stack(p_conv), jnp.stack(p_lconv), jnp.stack(p_h),
            st4(s_k), st4(s_v), s_conv.reshape(DEPTH, DB, CK - 1, CW), s_lconv.reshape(DEPTH, DB, LCK - 1, LW),
            s_h.reshape(DEPTH, DB, LW))
```

```python
import functools
import math

import jax
import jax.numpy as jnp
from jax import lax
from jax.experimental import pallas as pl
from jax.experimental.pallas import tpu as pltpu

F32 = jnp.float32
BF16 = jnp.bfloat16

D = 2048
B = 2
SEQ = 4096
DEPTH = 2
DB = 128
T = 8
PAST = 8192
N_META = 16
NH = 16
NKV = 4
G = NH // NKV
HD = 64
AW = NH * HD
KVW = NKV * HD
WIN = 128
BLK = 128
CW = D // 2
CK = 31
LW = D // 2
LCK = 4
LRU_C = 8.0
DFF = 5632
NE = 8
EPS = 1e-6
NEG = -1e30
IN_W = AW + 2 * KVW + 2 * CW + 2 * LW + 3 * D

PAD = (-N_META) % BLK
PB = PAD + N_META + SEQ
NB = PB // BLK
MP = B * PB
MS = DB * T
M = MP + MS

ZC = 512
NZC = IN_W // ZC
Z_Q, Z_A, Z_GATE, Z_LX, Z_LG, Z_GA, Z_K, Z_V = 0, 1024, 2048, 3072, 4096, 5120, 11264, 11520

SLOPES = tuple(2.0 ** (-8.0 * (h + 1.0) / NH) for h in range(NH))

TM_BIG = 1184
TM_MID = 592
TM_TOK = 256
SB = 8
MOE_TM = 256
MOE_TILES = (2 * M) // MOE_TM + NE
MOE_ROWS = MOE_TILES * MOE_TM
UP_TN = 1408
DOWN_TN = 1024

VMEM_LIMIT = 56 * 1024 * 1024


def _cp(*sem):
    return pltpu.CompilerParams(dimension_semantics=sem, vmem_limit_bytes=VMEM_LIMIT)


def _rms(x, g):
    return x * lax.rsqrt(jnp.mean(x * x, axis=-1, keepdims=True) + EPS) * g


def _sigmoid(x):
    return 1.0 / (1.0 + jnp.exp(-x))


def _silu(x):
    return x * _sigmoid(x)


def _norm_kernel(x_ref, g_ref, o_ref):
    o_ref[...] = _rms(x_ref[...], g_ref[...]).astype(o_ref.dtype)


def rmsnorm_cast(x, g):
    tm = TM_MID
    return pl.pallas_call(
        _norm_kernel,
        grid=(M // tm,),
        in_specs=[pl.BlockSpec((tm, D), lambda i: (i, 0)), pl.BlockSpec((1, D), lambda i: (0, 0))],
        out_specs=pl.BlockSpec((tm, D), lambda i: (i, 0)),
        out_shape=jax.ShapeDtypeStruct((M, D), BF16),
        compiler_params=_cp("parallel"),
        name="rmsnorm_cast",
    )(x, g.reshape(1, D))


def _inproj_kernel(x_ref, w_ref, o_ref):
    o_ref[...] = jnp.dot(x_ref[...], w_ref[...], preferred_element_type=F32).astype(o_ref.dtype)


def _z_src_block(j):
    return jnp.where(j < 2, j, jnp.where(j < NZC - 1, j + 1, 2))


def inproj(xn, w, layer):
    tm = TM_BIG
    return pl.pallas_call(
        _inproj_kernel,
        grid=(M // tm, NZC),
        in_specs=[pl.BlockSpec((tm, D), lambda i, j: (i, 0)),
                  pl.BlockSpec((None, D, ZC), lambda i, j: (layer, 0, _z_src_block(j)))],
        out_specs=pl.BlockSpec((tm, ZC), lambda i, j: (i, j)),
        out_shape=jax.ShapeDtypeStruct((M, IN_W), BF16),
        compiler_params=_cp("parallel", "arbitrary"),
        name="inproj",
    )(xn, w)


def _softmax_sink(s, sink):
    m = jnp.maximum(jnp.max(s, axis=-1, keepdims=True), sink)
    e = jnp.exp(s - m)
    l = jnp.sum(e, axis=-1, keepdims=True) + jnp.exp(sink - m)
    return e / l


def _attn_prompt_kernel(sink_ref, q_ref, kp_ref, kc_ref, vp_ref, vc_ref, o_ref):
    n = pl.program_id(1)
    q = q_ref[...]
    k = jnp.concatenate([kp_ref[...], kc_ref[...]], axis=0)
    v = jnp.concatenate([vp_ref[...], vc_ref[...]], axis=0)
    r = lax.broadcasted_iota(jnp.int32, (BLK, 2 * BLK), 0)
    c = lax.broadcasted_iota(jnp.int32, (BLK, 2 * BLK), 1)
    dist = BLK + r - c
    kpos = (n - 1) * BLK - PAD + c
    allowed = (dist >= 0) & (dist <= WIN) & (kpos >= 0)
    distf = dist.astype(F32)
    for h in range(NKV):
        kh = k[:, h * HD:(h + 1) * HD]
        vh = v[:, h * HD:(h + 1) * HD]
        qh = jnp.concatenate([q[:, (h * G + g) * HD:(h * G + g + 1) * HD] for g in range(G)], axis=0)
        s = lax.dot_general(qh, kh, (((1,), (1,)), ((), ())), preferred_element_type=F32)
        ps = []
        for g in range(G):
            hq = h * G + g
            sg = s[g * BLK:(g + 1) * BLK] * (HD ** -0.5) - SLOPES[hq] * distf
            sg = jnp.where(allowed, sg, NEG)
            ps.append(_softmax_sink(sg, sink_ref[hq]).astype(BF16))
        oh = jnp.dot(jnp.concatenate(ps, axis=0), vh, preferred_element_type=F32)
        for g in range(G):
            hq = h * G + g
            o_ref[:, hq * HD:(hq + 1) * HD] = oh[g * BLK:(g + 1) * BLK].astype(o_ref.dtype)


def attn_prompt(z, sinks):
    kb, vb = Z_K // KVW, Z_V // KVW

    def cur(col):
        return lambda b, n: (b * NB + n, col)

    def prev(col):
        return lambda b, n: (b * NB + jnp.maximum(n - 1, 0), col)

    return pl.pallas_call(
        _attn_prompt_kernel,
        grid=(B, NB),
        in_specs=[pl.BlockSpec(memory_space=pltpu.SMEM),
                  pl.BlockSpec((BLK, AW), cur(0)),
                  pl.BlockSpec((BLK, KVW), prev(kb)), pl.BlockSpec((BLK, KVW), cur(kb)),
                  pl.BlockSpec((BLK, KVW), prev(vb)), pl.BlockSpec((BLK, KVW), cur(vb))],
        out_specs=pl.BlockSpec((BLK, AW), cur(0)),
        out_shape=jax.ShapeDtypeStruct((MP, AW), BF16),
        compiler_params=_cp("parallel", "arbitrary"),
        name="attn_prompt",
    )(sinks, z, z, z, z, z)


def _attn_sample_kernel(sink_ref, q_ref, kn_ref, vn_ref, ck_ref, cv_ref, nk_in_ref, nv_in_ref,
                        o_ref, nk_ref, nv_ref):
    del nk_in_ref, nv_in_ref
    q = q_ref[...].astype(F32).reshape(SB, T, AW)
    kn = kn_ref[...].astype(F32).reshape(SB, T, KVW)
    vn = vn_ref[...].astype(F32).reshape(SB, T, KVW)
    ck = ck_ref[...]
    cv = cv_ref[...]
    wb = WIN
    nk_ref[:, 0:wb - T, :] = ck[:, T:wb, :]
    nk_ref[:, wb - T:wb, :] = kn
    nv_ref[:, 0:wb - T, :] = cv[:, T:wb, :]
    nv_ref[:, wb - T:wb, :] = vn
    zpad = jnp.zeros((SB, wb - T, KVW), F32)
    k = jnp.concatenate([ck, kn, zpad], axis=1).astype(BF16)
    v = jnp.concatenate([cv, vn, zpad], axis=1).astype(BF16)
    nkeys = 2 * wb
    r = lax.broadcasted_iota(jnp.int32, (G * T, nkeys), 0)
    c = lax.broadcasted_iota(jnp.int32, (G * T, nkeys), 1)
    dist = wb + (r % T) - c
    allowed = (dist >= 0) & (dist <= WIN)
    distf = dist.astype(F32)
    gidx = r // T
    for h in range(NKV):
        slope = jnp.zeros((G * T, nkeys), F32)
        sink = jnp.zeros((G * T, 1), F32)
        for g in range(G):
            slope = jnp.where(gidx == g, SLOPES[h * G + g], slope)
            sink = jnp.where(gidx[:, 0:1] == g, sink_ref[h * G + g], sink)
        kh = k[:, :, h * HD:(h + 1) * HD]
        vh = v[:, :, h * HD:(h + 1) * HD]
        qh = jnp.concatenate([q[:, :, (h * G + g) * HD:(h * G + g + 1) * HD] for g in range(G)],
                             axis=1).astype(BF16)
        s = jnp.einsum("bqd,bkd->bqk", qh, kh, preferred_element_type=F32)
        s = s * (HD ** -0.5) - (slope * distf)[None]
        s = jnp.where(allowed[None], s, NEG)
        p = _softmax_sink(s, sink[None]).astype(BF16)
        oh = jnp.einsum("bqk,bkd->bqd", p, vh, preferred_element_type=F32)
        for g in range(G):
            hq = h * G + g
            piece = oh[:, g * T:(g + 1) * T, :].reshape(SB * T, HD)
            o_ref[:, hq * HD:(hq + 1) * HD] = piece.astype(o_ref.dtype)


def attn_sample(z, sinks, cache_k, cache_v, nk, nv, layer):
    rows = SB * T
    rb0 = MP // rows
    kb, vb = Z_K // KVW, Z_V // KVW
    cb0 = layer * (DB // SB)
    cspec = pl.BlockSpec((SB, WIN, KVW), lambda i: (cb0 + i, 0, 0))
    anyspec = pl.BlockSpec(memory_space=pl.ANY)
    return pl.pallas_call(
        _attn_sample_kernel,
        grid=(DB // SB,),
        in_specs=[pl.BlockSpec(memory_space=pltpu.SMEM),
                  pl.BlockSpec((rows, AW), lambda i: (rb0 + i, 0)),
                  pl.BlockSpec((rows, KVW), lambda i: (rb0 + i, kb)),
                  pl.BlockSpec((rows, KVW), lambda i: (rb0 + i, vb)),
                  cspec, cspec, anyspec, anyspec],
        out_specs=[pl.BlockSpec((rows, AW), lambda i: (i, 0)), cspec, cspec],
        out_shape=[jax.ShapeDtypeStruct((MS, AW), BF16),
                   jax.ShapeDtypeStruct((DEPTH * DB, WIN, KVW), F32),
                   jax.ShapeDtypeStruct((DEPTH * DB, WIN, KVW), F32)],
        input_output_aliases={6: 1, 7: 2},
        compiler_params=_cp("parallel"),
        name="attn_sample",
    )(sinks, z, z, z, cache_k, cache_v, nk, nv)


def _ln_swish(y, g, b):
    mu = jnp.mean(y, axis=-1, keepdims=True)
    yc = y - mu
    var = jnp.mean(yc * yc, axis=-1, keepdims=True)
    yn = yc * lax.rsqrt(var + EPS) * g + b
    return _silu(yn)


CONV_HIST = 32
SUBLANES = 8


def _dwconv_by_phase(load, wrow, nrows, first, ntaps):
    y = None
    for s in range(SUBLANES):
        n = nrows if s == 0 else nrows + SUBLANES
        part = None
        for q in range((first + ntaps - 1) // SUBLANES + 1):
            o = SUBLANES * q + s
            if first <= o < first + ntaps:
                term = wrow(o - first) * load(SUBLANES * q, n)
                part = term if part is None else part + term
        if part is not None:
            part = part[s:s + nrows]
            y = part if y is None else y + part
    return y


def _conv_prompt_kernel(a_ref, g_ref, w_ref, b_ref, lg_ref, lb_ref, c_ref, st_ref, ext_ref, y_ref):
    n = pl.program_id(1)

    @pl.when(n == 0)
    def _():
        ext_ref[0:CONV_HIST, :] = jnp.zeros((CONV_HIST, CW), F32)

    @pl.when(n > 0)
    def _():
        ext_ref[0:CONV_HIST, :] = ext_ref[BLK:BLK + CONV_HIST, :]

    u = a_ref[...].astype(F32) * _sigmoid(g_ref[...].astype(F32))
    row = lax.broadcasted_iota(jnp.int32, (BLK, 1), 0)
    u = jnp.where((n > 0) | (row >= PAD), u, 0.0)
    ext_ref[CONV_HIST:CONV_HIST + BLK, :] = u
    off = CONV_HIST - (CK - 1)
    cc = 256
    for c0 in range(0, CW, cc):
        cs = slice(c0, c0 + cc)
        y_ref[:, cs] = b_ref[:, cs] + _dwconv_by_phase(lambda r0, nr: ext_ref[r0:r0 + nr, cs],
                                                      lambda j: w_ref[j:j + 1, cs], BLK, off, CK)
    c_ref[...] = _ln_swish(y_ref[...], lg_ref[...], lb_ref[...]).astype(c_ref.dtype)

    @pl.when(n == NB - 1)
    def _():
        st_ref[0] = ext_ref[CONV_HIST + BLK - (CK - 1):CONV_HIST + BLK, :]


def conv_prompt(z, w, b, lg, lb):
    vec = pl.BlockSpec((1, CW), lambda bb, n: (0, 0))
    return pl.pallas_call(
        _conv_prompt_kernel,
        grid=(B, NB),
        in_specs=[pl.BlockSpec((BLK, CW), lambda bb, n: (bb * NB + n, Z_A // CW)),
                  pl.BlockSpec((BLK, CW), lambda bb, n: (bb * NB + n, Z_GATE // CW)),
                  pl.BlockSpec((CK, CW), lambda bb, n: (0, 0)), vec, vec, vec],
        out_specs=[pl.BlockSpec((BLK, CW), lambda bb, n: (bb * NB + n, 0)),
                   pl.BlockSpec((1, CK - 1, CW), lambda bb, n: (bb, 0, 0))],
        out_shape=[jax.ShapeDtypeStruct((MP, CW), BF16), jax.ShapeDtypeStruct((B, CK - 1, CW), F32)],
        scratch_shapes=[pltpu.VMEM((CONV_HIST + BLK, CW), F32), pltpu.VMEM((BLK, CW), F32)],
        compiler_params=_cp("parallel", "arbitrary"),
        name="conv_prompt",
    )(z, z, w, b.reshape(1, CW), lg.reshape(1, CW), lb.reshape(1, CW))


def _conv_sample_kernel(a_ref, g_ref, st_ref, w_ref, b_ref, lg_ref, lb_ref, nst_in_ref,
                        c_ref, nst_ref, ext_ref, y_ref):
    del nst_in_ref
    u = a_ref[...].astype(F32) * _sigmoid(g_ref[...].astype(F32))
    hist = CK - 1
    base = CONV_HIST - hist
    for s in range(SB):
        ext_ref[s, 0:base, :] = jnp.zeros((base, CW), F32)
        ext_ref[s, base:CONV_HIST, :] = st_ref[s]
        ext_ref[s, CONV_HIST:CONV_HIST + T, :] = u[s * T:(s + 1) * T]
    for s in range(SB):
        y_ref[s * T:(s + 1) * T, :] = b_ref[...] + _dwconv_by_phase(
            lambda r0, nr: ext_ref[s, r0:r0 + nr, :], lambda j: w_ref[j:j + 1, :], T, base, CK)
        nst_ref[s] = ext_ref[s, base + T:base + T + hist, :]
    c_ref[...] = _ln_swish(y_ref[...], lg_ref[...], lb_ref[...]).astype(c_ref.dtype)


def conv_sample(z, state, w, b, lg, lb, nst, layer):
    rows = SB * T
    rb0 = MP // rows
    sb0 = layer * (DB // SB)
    vec = pl.BlockSpec((1, CW), lambda i: (0, 0))
    sspec = pl.BlockSpec((SB, CK - 1, CW), lambda i: (sb0 + i, 0, 0))
    anyspec = pl.BlockSpec(memory_space=pl.ANY)
    return pl.pallas_call(
        _conv_sample_kernel,
        grid=(DB // SB,),
        in_specs=[pl.BlockSpec((rows, CW), lambda i: (rb0 + i, Z_A // CW)),
                  pl.BlockSpec((rows, CW), lambda i: (rb0 + i, Z_GATE // CW)),
                  sspec, pl.BlockSpec((CK, CW), lambda i: (0, 0)), vec, vec, vec, anyspec],
        out_specs=[pl.BlockSpec((rows, CW), lambda i: (i, 0)), sspec],
        out_shape=[jax.ShapeDtypeStruct((MS, CW), BF16), jax.ShapeDtypeStruct((DEPTH * DB, CK - 1, CW), F32)],
        scratch_shapes=[pltpu.VMEM((SB, CONV_HIST + T, CW), F32), pltpu.VMEM((rows, CW), F32)],
        input_output_aliases={7: 1},
        compiler_params=_cp("parallel"),
        name="conv_sample",
    )(z, z, state, w, b.reshape(1, CW), lg.reshape(1, CW), lb.reshape(1, CW), nst)


LRU_HIST = 8
GATE_CH = 256


def _gelu_tanh(x):
    return x * (0.5 * (1.0 + jnp.tanh(math.sqrt(2.0 / math.pi) * (x + 0.044715 * (x * x * x)))))


def _softplus(x):
    return jnp.maximum(x, 0.0) + jnp.log1p(jnp.exp(-jnp.abs(x)))


def _expm1(x):
    return jnp.tanh(0.5 * x) * (jnp.exp(x) + 1.0)


def _lru_gates(xc, wr_ref, br_ref, wi_ref, bi_ref, lam_ref, valid, a_ref, bx_ref):
    sp = _softplus(-lam_ref[...])
    for k in range(LW // GATE_CH):
        sl = slice(k * GATE_CH, (k + 1) * GATE_CH)
        xk = xc[:, sl]
        xkb = xk.astype(BF16)
        r = _sigmoid(jnp.dot(xkb, wr_ref[k], preferred_element_type=F32) + br_ref[:, sl])
        i = _sigmoid(jnp.dot(xkb, wi_ref[k], preferred_element_type=F32) + bi_ref[:, sl])
        log_a = (-LRU_C) * r * sp[:, sl]
        a = jnp.exp(log_a)
        bx = jnp.sqrt(-_expm1(2.0 * log_a)) * (i * xk)
        if valid is not None:
            bx = jnp.where(valid, bx, 0.0)
        a_ref[:, sl] = a
        bx_ref[:, sl] = bx


def _lru_prompt_kernel(x_ref, g_ref, cw_ref, cb_ref, wr_ref, br_ref, wi_ref, bi_ref, lam_ref,
                       rr_ref, cst_ref, hst_ref, ext_ref, a_ref, bx_ref, h_ref, carry_ref):
    n = pl.program_id(1)

    @pl.when(n == 0)
    def _():
        ext_ref[0:LRU_HIST, :] = jnp.zeros((LRU_HIST, LW), F32)
        carry_ref[...] = jnp.zeros((1, LW), F32)

    @pl.when(n > 0)
    def _():
        ext_ref[0:LRU_HIST, :] = ext_ref[BLK:BLK + LRU_HIST, :]

    row = lax.broadcasted_iota(jnp.int32, (BLK, 1), 0)
    valid = (n > 0) | (row >= PAD)
    ext_ref[LRU_HIST:LRU_HIST + BLK, :] = jnp.where(valid, x_ref[...].astype(F32), 0.0)
    off = LRU_HIST - (LCK - 1)
    xc = jnp.zeros((BLK, LW), F32) + cb_ref[...]
    for j in range(LCK):
        xc = xc + cw_ref[j:j + 1, :] * ext_ref[off + j:off + j + BLK, :]
    _lru_gates(xc, wr_ref, br_ref, wi_ref, bi_ref, lam_ref, valid, a_ref, bx_ref)

    def step(t, h):
        h = a_ref[pl.ds(t, 1), :] * h + bx_ref[pl.ds(t, 1), :]
        h_ref[pl.ds(t, 1), :] = h
        return h

    h = lax.fori_loop(0, BLK, step, carry_ref[...], unroll=8)
    carry_ref[...] = h
    rr_ref[...] = (h_ref[...] * _gelu_tanh(g_ref[...].astype(F32))).astype(rr_ref.dtype)

    @pl.when(n == NB - 1)
    def _():
        cst_ref[0] = ext_ref[LRU_HIST + BLK - (LCK - 1):LRU_HIST + BLK, :]
        hst_ref[0] = h


def lru_prompt(z, cw, cb, wr, br, wi, bi, lam):
    vec = pl.BlockSpec((1, LW), lambda bb, n: (0, 0))
    wspec = pl.BlockSpec((LW // GATE_CH, GATE_CH, GATE_CH), lambda bb, n: (0, 0, 0))
    return pl.pallas_call(
        _lru_prompt_kernel,
        grid=(B, NB),
        in_specs=[pl.BlockSpec((BLK, LW), lambda bb, n: (bb * NB + n, Z_LX // LW)),
                  pl.BlockSpec((BLK, LW), lambda bb, n: (bb * NB + n, Z_LG // LW)),
                  pl.BlockSpec((LCK, LW), lambda bb, n: (0, 0)), vec, wspec, vec, wspec, vec, vec],
        out_specs=[pl.BlockSpec((BLK, LW), lambda bb, n: (bb * NB + n, 0)),
                   pl.BlockSpec((1, LCK - 1, LW), lambda bb, n: (bb, 0, 0)),
                   pl.BlockSpec((1, 1, LW), lambda bb, n: (bb, 0, 0))],
        out_shape=[jax.ShapeDtypeStruct((MP, LW), BF16), jax.ShapeDtypeStruct((B, LCK - 1, LW), F32),
                   jax.ShapeDtypeStruct((B, 1, LW), F32)],
        scratch_shapes=[pltpu.VMEM((LRU_HIST + BLK, LW), F32), pltpu.VMEM((BLK, LW), F32),
                        pltpu.VMEM((BLK, LW), F32), pltpu.VMEM((BLK, LW), F32), pltpu.VMEM((1, LW), F32)],
        compiler_params=_cp("parallel", "arbitrary"),
        name="lru_prompt",
    )(z, z, cw, cb.reshape(1, LW), wr, br.reshape(1, LW), wi, bi.reshape(1, LW), lam.reshape(1, LW))


def _lru_sample_kernel(x_ref, g_ref, cst_ref, h0_ref, cw_ref, cb_ref, wr_ref, br_ref, wi_ref, bi_ref, lam_ref,
                       ncst_in_ref, nh_in_ref, rr_ref, ncst_ref, nh_ref,
                       ext_ref, xc_ref, a_ref, bx_ref, h_ref):
    del ncst_in_ref, nh_in_ref
    hist = LCK - 1
    base = LRU_HIST - hist
    x = x_ref[...].astype(F32)
    for s in range(SB):
        ext_ref[s, base:LRU_HIST, :] = cst_ref[s]
        ext_ref[s, LRU_HIST:LRU_HIST + T, :] = x[s * T:(s + 1) * T]
    for s in range(SB):
        acc = jnp.zeros((T, LW), F32) + cb_ref[...]
        for j in range(LCK):
            acc = acc + cw_ref[j:j + 1, :] * ext_ref[s, base + j:base + j + T, :]
        xc_ref[s * T:(s + 1) * T, :] = acc
        ncst_ref[s] = ext_ref[s, base + T:base + T + hist, :]
    _lru_gates(xc_ref[...], wr_ref, br_ref, wi_ref, bi_ref, lam_ref, None, a_ref, bx_ref)
    for s in range(SB):
        h = h0_ref[s:s + 1, :]
        for t in range(T):
            rw = s * T + t
            h = a_ref[rw:rw + 1, :] * h + bx_ref[rw:rw + 1, :]
            h_ref[rw:rw + 1, :] = h
        nh_ref[s:s + 1, :] = h
    rr_ref[...] = (h_ref[...] * _gelu_tanh(g_ref[...].astype(F32))).astype(rr_ref.dtype)


def lru_sample(z, cstate, hstate, cw, cb, wr, br, wi, bi, lam, ncst, nh, layer):
    rows = SB * T
    rb0 = MP // rows
    sb0 = layer * (DB // SB)
    vec = pl.BlockSpec((1, LW), lambda i: (0, 0))
    wspec = pl.BlockSpec((LW // GATE_CH, GATE_CH, GATE_CH), lambda i: (0, 0, 0))
    cspec = pl.BlockSpec((SB, LCK - 1, LW), lambda i: (sb0 + i, 0, 0))
    hspec = pl.BlockSpec((SB, LW), lambda i: (sb0 + i, 0))
    anyspec = pl.BlockSpec(memory_space=pl.ANY)
    return pl.pallas_call(
        _lru_sample_kernel,
        grid=(DB // SB,),
        in_specs=[pl.BlockSpec((rows, LW), lambda i: (rb0 + i, Z_LX // LW)),
                  pl.BlockSpec((rows, LW), lambda i: (rb0 + i, Z_LG // LW)),
                  cspec, hspec, pl.BlockSpec((LCK, LW), lambda i: (0, 0)), vec, wspec, vec, wspec, vec, vec,
                  anyspec, anyspec],
        out_specs=[pl.BlockSpec((rows, LW), lambda i: (i, 0)), cspec, hspec],
        out_shape=[jax.ShapeDtypeStruct((MS, LW), BF16), jax.ShapeDtypeStruct((DEPTH * DB, LCK - 1, LW), F32),
                   jax.ShapeDtypeStruct((DEPTH * DB, LW), F32)],
        scratch_shapes=[pltpu.VMEM((SB, LRU_HIST + T, LW), F32)] + [pltpu.VMEM((rows, LW), F32)] * 4,
        input_output_aliases={11: 1, 12: 2},
        compiler_params=_cp("parallel"),
        name="lru_sample",
    )(z, z, cstate, hstate, cw, cb.reshape(1, LW), wr, br.reshape(1, LW), wi, bi.reshape(1, LW),
      lam.reshape(1, LW), ncst, nh)


ROUTE_W = 128


def _mix_body(branch_refs, gates, x_ref, wa_ref, wc_ref, wl_ref, wm_ref, g_ref, x1_ref):
    is_prompt = pl.program_id(0) < MP // TM_TOK
    o, c, r = (jnp.where(is_prompt, p_ref[...], s_ref[...]) for p_ref, s_ref in branch_refs)
    half = D // 2
    parts = []
    for hh in range(2):
        sl = slice(hh * half, (hh + 1) * half)
        m = _sigmoid(gates[0][hh][...].astype(F32)) * jnp.dot(o, wa_ref[:, sl], preferred_element_type=F32)
        m = m + _sigmoid(gates[1][hh][...].astype(F32)) * jnp.dot(c, wc_ref[:, sl], preferred_element_type=F32)
        m = m + _sigmoid(gates[2][hh][...].astype(F32)) * jnp.dot(r, wl_ref[:, sl], preferred_element_type=F32)
        parts.append(m.astype(BF16))
    merged = jnp.concatenate(parts, axis=1)
    x1 = x_ref[...] + jnp.dot(merged, wm_ref[...], preferred_element_type=F32)
    x1_ref[...] = x1
    return _rms(x1, g_ref[...])


def _mix_kernel(op_ref, os_ref, cp_ref, cs_ref, rp_ref, rs_ref, ga0, ga1, gb0, gb1, gc0, gc1, x_ref,
                wa_ref, wc_ref, wl_ref, wm_ref, g_ref, x1_ref, hn_ref):
    hn = _mix_body(((op_ref, os_ref), (cp_ref, cs_ref), (rp_ref, rs_ref)), ((ga0, ga1), (gb0, gb1), (gc0, gc1)),
                   x_ref, wa_ref, wc_ref, wl_ref, wm_ref, g_ref, x1_ref)
    hn_ref[...] = hn.astype(hn_ref.dtype)


def _mix_router_kernel(op_ref, os_ref, cp_ref, cs_ref, rp_ref, rs_ref, ga0, ga1, gb0, gb1, gc0, gc1, x_ref,
                       wa_ref, wc_ref, wl_ref, wm_ref, g_ref, rw_ref, rb_ref, x1_ref, route_ref):
    hn = _mix_body(((op_ref, os_ref), (cp_ref, cs_ref), (rp_ref, rs_ref)), ((ga0, ga1), (gb0, gb1), (gc0, gc1)),
                   x_ref, wa_ref, wc_ref, wl_ref, wm_ref, g_ref, x1_ref)
    h_hi = hn.astype(BF16)
    h_lo = (hn - h_hi.astype(F32)).astype(BF16)
    both = jnp.dot(h_hi, rw_ref[...], preferred_element_type=F32)
    logits = (both[:, :ROUTE_W] + both[:, ROUTE_W:]
              + jnp.dot(h_lo, rw_ref[:, :ROUTE_W], preferred_element_type=F32) + rb_ref[...])
    lane = lax.broadcasted_iota(jnp.int32, logits.shape, 1)
    ninf = -jnp.inf
    l1 = jnp.where(lane < NE, logits, ninf)
    m1 = jnp.max(l1, axis=-1, keepdims=True)
    i1 = jnp.min(jnp.where(l1 == m1, lane, ROUTE_W), axis=-1, keepdims=True)
    l2 = jnp.where(lane == i1, ninf, l1)
    m2 = jnp.max(l2, axis=-1, keepdims=True)
    i2 = jnp.min(jnp.where(l2 == m2, lane, ROUTE_W), axis=-1, keepdims=True)
    e2 = jnp.exp(m2 - m1)
    den = 1.0 + e2
    out = jnp.where(lane == 0, i1.astype(F32), 0.0)
    out = jnp.where(lane == 1, i2.astype(F32), out)
    out = jnp.where(lane == 2, 1.0 / den, out)
    out = jnp.where(lane == 3, e2 / den, out)
    route_ref[...] = out


def mix(o, c, rr, z, x, wa, wc, wl, wm, g, layer, router=None):
    tm = TM_TOK
    half = D // 2
    npt = MP // tm
    row = lambda col: (lambda i: (i, col))
    const = lambda shape: pl.BlockSpec(shape, lambda i: (0,) * len(shape), pipeline_mode=pl.Buffered(1))
    wspec = lambda k: pl.BlockSpec((None, k, D), lambda i: (layer, 0, 0), pipeline_mode=pl.Buffered(1))
    gate_specs = [pl.BlockSpec((tm, half), row(Z_GA // half + k)) for k in range(6)]
    pair = lambda w: [pl.BlockSpec((tm, w), lambda i: (jnp.minimum(i, npt - 1), 0)),
                      pl.BlockSpec((tm, w), lambda i: (jnp.maximum(i - npt, 0), 0))]
    in_specs = (pair(AW) + pair(CW) + pair(LW) + gate_specs
                + [pl.BlockSpec((tm, D), row(0)), wspec(AW), wspec(CW), wspec(LW), wspec(D), const((1, D))])
    args = [*o, *c, *rr, z, z, z, z, z, z, x, wa, wc, wl, wm, g.reshape(1, D)]
    if router is None:
        kern = _mix_kernel
        out_specs = [pl.BlockSpec((tm, D), row(0)), pl.BlockSpec((tm, D), row(0))]
        out_shape = [jax.ShapeDtypeStruct((M, D), F32), jax.ShapeDtypeStruct((M, D), BF16)]
    else:
        kern = _mix_router_kernel
        rw, rb = router
        in_specs += [const((D, 2 * ROUTE_W)), const((1, ROUTE_W))]
        args += [rw, rb]
        out_specs = [pl.BlockSpec((tm, D), row(0)), pl.BlockSpec((tm, ROUTE_W), row(0))]
        out_shape = [jax.ShapeDtypeStruct((M, D), F32), jax.ShapeDtypeStruct((M, ROUTE_W), F32)]
    return pl.pallas_call(
        kern,
        grid=(M // tm,),
        in_specs=in_specs,
        out_specs=out_specs,
        out_shape=out_shape,
        compiler_params=_cp("parallel"),
        name="mix",
    )(*args)


def _ffn_up_kernel(x_ref, w1_ref, w3_ref, o_ref):
    x = x_ref[...]
    a = jnp.dot(x, w1_ref[...], preferred_element_type=F32)
    b = jnp.dot(x, w3_ref[...], preferred_element_type=F32)
    o_ref[...] = (_silu(a) * b).astype(o_ref.dtype)


def ffn_up(hn, w1, w3, idx):
    tm, tn = TM_BIG, 512
    wspec = pl.BlockSpec((None, D, tn), lambda i, j: (idx, 0, j))
    return pl.pallas_call(
        _ffn_up_kernel,
        grid=(M // tm, DFF // tn),
        in_specs=[pl.BlockSpec((tm, D), lambda i, j: (i, 0)), wspec, wspec],
        out_specs=pl.BlockSpec((tm, tn), lambda i, j: (i, j)),
        out_shape=jax.ShapeDtypeStruct((M, DFF), BF16),
        compiler_params=_cp("parallel", "arbitrary"),
        name="ffn_up",
    )(hn, w1, w3)


def _ffn_down_kernel(h_ref, w_ref, x_ref, g_ref, x2_ref, xn_ref):
    x2 = x_ref[...] + jnp.dot(h_ref[...], w_ref[...], preferred_element_type=F32)
    x2_ref[...] = x2
    xn_ref[...] = _rms(x2, g_ref[...]).astype(xn_ref.dtype)


def ffn_down(h, w2, x1, g, idx):
    tm = TM_TOK
    return pl.pallas_call(
        _ffn_down_kernel,
        grid=(M // tm,),
        in_specs=[pl.BlockSpec((tm, DFF), lambda i: (i, 0)),
                  pl.BlockSpec((None, DFF, D), lambda i: (idx, 0, 0), pipeline_mode=pl.Buffered(1)),
                  pl.BlockSpec((tm, D), lambda i: (i, 0)), pl.BlockSpec((1, D), lambda i: (0, 0))],
        out_specs=[pl.BlockSpec((tm, D), lambda i: (i, 0)), pl.BlockSpec((tm, D), lambda i: (i, 0))],
        out_shape=[jax.ShapeDtypeStruct((M, D), F32), jax.ShapeDtypeStruct((M, D), BF16)],
        compiler_params=_cp("parallel"),
        name="ffn_down",
    )(h, w2, x1, g.reshape(1, D))


def _row_copy(src, src_row, dst, dst_row, sem):
    return pltpu.make_async_copy(src.at[pl.ds(src_row, 1)], dst.at[pl.ds(dst_row, 1)], sem)


def _moe_scatter_kernel(s1_ref, s2_ref, zt_ref, x_ref, g_ref, xs_ref, buf_ref, zero_ref, sem):
    i = pl.program_id(0)
    tm = MOE_TM

    def tile_copy(e):
        return pltpu.make_async_copy(zero_ref, xs_ref.at[pl.ds(pl.multiple_of(zt_ref[e] * tm, tm), tm)], sem)

    @pl.when(i == 0)
    def _():
        zero_ref[...] = jnp.zeros(zero_ref.shape, F32)
        for e in range(2 * NE):
            tile_copy(e).start()
            tile_copy(e).wait()

    buf_ref[...] = _rms(x_ref[...], g_ref[...])

    def issue(r, carry):
        t = i * tm + r
        _row_copy(buf_ref, r, xs_ref, s1_ref[t], sem).start()
        _row_copy(buf_ref, r, xs_ref, s2_ref[t], sem).start()
        return carry

    lax.fori_loop(0, tm, issue, 0)

    def drain(r, carry):
        _row_copy(buf_ref, 0, xs_ref, 0, sem).wait()
        _row_copy(buf_ref, 0, xs_ref, 0, sem).wait()
        return carry

    lax.fori_loop(0, tm, drain, 0)


def moe_scatter(x1, g, slot1, slot2, zero_tiles):
    tm = MOE_TM
    return pl.pallas_call(
        _moe_scatter_kernel,
        grid_spec=pltpu.PrefetchScalarGridSpec(
            num_scalar_prefetch=3,
            grid=(M // tm,),
            in_specs=[pl.BlockSpec((tm, D), lambda i, *_: (i, 0)), pl.BlockSpec((1, D), lambda i, *_: (0, 0))],
            out_specs=pl.BlockSpec(memory_space=pl.ANY),
            scratch_shapes=[pltpu.VMEM((tm, D), F32), pltpu.VMEM((tm, D), F32), pltpu.SemaphoreType.DMA(())]),
        out_shape=jax.ShapeDtypeStruct((MOE_ROWS, D), F32),
        compiler_params=_cp("arbitrary"),
        name="moe_scatter",
    )(slot1, slot2, zero_tiles, x1, g.reshape(1, D))


def _moe_up_kernel(te_ref, nu_ref, x_ref, w1_ref, w3_ref, o_ref):
    used = pl.program_id(1) < nu_ref[0]

    @pl.when(used)
    def _():
        x = x_ref[...].astype(BF16)
        a = jnp.dot(x, w1_ref[0], preferred_element_type=F32)
        b = jnp.dot(x, w3_ref[0], preferred_element_type=F32)
        o_ref[...] = (_silu(a) * b).astype(o_ref.dtype)

    @pl.when(jnp.logical_not(used))
    def _():
        o_ref[...] = jnp.zeros(o_ref.shape, o_ref.dtype)


def _used(i, nu_ref):
    return jnp.minimum(i, nu_ref[0] - 1)


def moe_up(xs, w1, w3, tile_expert, n_used, idx):
    tm, tn = MOE_TM, UP_TN
    wspec = pl.BlockSpec((None, 1, D, tn), lambda j, i, te, nu: (idx, te[_used(i, nu)], 0, j))
    return pl.pallas_call(
        _moe_up_kernel,
        grid_spec=pltpu.PrefetchScalarGridSpec(
            num_scalar_prefetch=2,
            grid=(DFF // tn, MOE_TILES),
            in_specs=[pl.BlockSpec((tm, D), lambda j, i, te, nu: (_used(i, nu), 0)), wspec, wspec],
            out_specs=pl.BlockSpec((tm, tn), lambda j, i, te, nu: (i, j))),
        out_shape=jax.ShapeDtypeStruct((MOE_ROWS, DFF), BF16),
        compiler_params=_cp("arbitrary", "arbitrary"),
        name="moe_up",
    )(tile_expert, n_used, xs, w1, w3)


def _moe_down_kernel(te_ref, nu_ref, h_ref, w_ref, o_ref):
    used = pl.program_id(1) < nu_ref[0]

    @pl.when(used)
    def _():
        o_ref[...] = jnp.dot(h_ref[...], w_ref[0], preferred_element_type=F32)

    @pl.when(jnp.logical_not(used))
    def _():
        o_ref[...] = jnp.zeros(o_ref.shape, o_ref.dtype)


def moe_down(hs, w2, tile_expert, n_used, idx):
    tm, tn = MOE_TM, DOWN_TN
    return pl.pallas_call(
        _moe_down_kernel,
        grid_spec=pltpu.PrefetchScalarGridSpec(
            num_scalar_prefetch=2,
            grid=(D // tn, MOE_TILES),
            in_specs=[pl.BlockSpec((tm, DFF), lambda j, i, te, nu: (_used(i, nu), 0)),
                      pl.BlockSpec((None, 1, DFF, tn), lambda j, i, te, nu: (idx, te[_used(i, nu)], 0, j))],
            out_specs=pl.BlockSpec((tm, tn), lambda j, i, te, nu: (i, j))),
        out_shape=jax.ShapeDtypeStruct((MOE_ROWS, D), F32),
        compiler_params=_cp("arbitrary", "arbitrary"),
        name="moe_down",
    )(tile_expert, n_used, hs, w2)


COMB_TM = BLK


def _moe_combine_kernel(s1_ref, s2_ref, x_ref, route_ref, g_ref, ys_ref, yp_ref, ysm_ref, ya_ref, yb_ref, sem):
    i = pl.program_id(0)
    tm = COMB_TM

    def issue(r, carry):
        t = i * tm + r
        _row_copy(ys_ref, s1_ref[t], ya_ref, r, sem).start()
        _row_copy(ys_ref, s2_ref[t], yb_ref, r, sem).start()
        return carry

    lax.fori_loop(0, tm, issue, 0)

    def drain(r, carry):
        _row_copy(ys_ref, 0, ya_ref, 0, sem).wait()
        _row_copy(ys_ref, 0, yb_ref, 0, sem).wait()
        return carry

    lax.fori_loop(0, tm, drain, 0)
    w1 = route_ref[:, 2:3]
    w2 = route_ref[:, 3:4]
    x2 = x_ref[...] + (w1 * ya_ref[...] + w2 * yb_ref[...])
    y = _rms(x2, g_ref[...])
    is_prompt = i < B * NB

    @pl.when(is_prompt & (i % NB > 0))
    def _():
        yp_ref[...] = y

    @pl.when(jnp.logical_not(is_prompt))
    def _():
        ysm_ref[...] = y


def _yp_block(i):
    ic = jnp.minimum(i, B * NB - 1)
    return (ic // NB) * (NB - 1) + jnp.maximum(ic % NB - 1, 0)


def moe_combine(x1, route, g, ys, slot1, slot2):
    tm = COMB_TM
    return pl.pallas_call(
        _moe_combine_kernel,
        grid_spec=pltpu.PrefetchScalarGridSpec(
            num_scalar_prefetch=2,
            grid=(M // tm,),
            in_specs=[pl.BlockSpec((tm, D), lambda i, *_: (i, 0)), pl.BlockSpec((tm, ROUTE_W), lambda i, *_: (i, 0)),
                      pl.BlockSpec((1, D), lambda i, *_: (0, 0)), pl.BlockSpec(memory_space=pl.ANY)],
            out_specs=[pl.BlockSpec((tm, D), lambda i, *_: (_yp_block(i), 0)),
                       pl.BlockSpec((tm, D), lambda i, *_: (jnp.maximum(i - B * NB, 0), 0))],
            scratch_shapes=[pltpu.VMEM((tm, D), F32), pltpu.VMEM((tm, D), F32), pltpu.SemaphoreType.DMA(())]),
        out_shape=[jax.ShapeDtypeStruct((B * SEQ, D), F32), jax.ShapeDtypeStruct((MS, D), F32)],
        compiler_params=_cp("arbitrary"),
        name="moe_combine",
    )(slot1, slot2, x1, route, g.reshape(1, D), ys)


def _moe_plan(route):
    e1 = route[:, 0].astype(jnp.int32)
    e2 = route[:, 1].astype(jnp.int32)
    ids = jnp.arange(NE, dtype=jnp.int32)
    sel = (e1[:, None] == ids).astype(jnp.int32) + (e2[:, None] == ids).astype(jnp.int32)
    incl = jnp.cumsum(sel, axis=0)
    rank = incl - sel
    cnt = incl[-1]
    ntile = (cnt + MOE_TM - 1) // MOE_TM
    tile_end = jnp.cumsum(ntile)
    tile_off = tile_end - ntile
    row_off = tile_off * MOE_TM
    slot1 = jnp.take(row_off, e1) + jnp.take_along_axis(rank, e1[:, None], axis=1)[:, 0]
    slot2 = jnp.take(row_off, e2) + jnp.take_along_axis(rank, e2[:, None], axis=1)[:, 0]
    tiles = jnp.arange(MOE_TILES, dtype=jnp.int32)
    tile_expert = jnp.minimum(jnp.sum((tiles[:, None] >= tile_end[None, :]).astype(jnp.int32), axis=1), NE - 1)
    n_used = tile_end[-1:].astype(jnp.int32)
    tail = jnp.minimum(tile_end[-1] + ids, MOE_TILES - 1)
    zero_tiles = jnp.concatenate([jnp.maximum(tile_end - 1, 0), tail]).astype(jnp.int32)
    return slot1.astype(jnp.int32), slot2.astype(jnp.int32), tile_expert.astype(jnp.int32), n_used, zero_tiles


def _blockdiag(w):
    per = GATE_CH // (LW // 16)
    w4 = w.reshape(LW // GATE_CH, per, 64, 64)
    eye = jnp.eye(per, dtype=w.dtype)
    return jnp.einsum("cnde,nm->cndme", w4, eye).reshape(LW // GATE_CH, GATE_CH, GATE_CH).astype(BF16)


def kernel(x_prompt, x_sample, cache_attn_k, cache_attn_v, state_conv, state_lru_conv, state_lru_h, meta_tokens,
           norm_mix, norm_ffn, norm_final, w_in, attn_sinks, w_attn_out, conv_dw_w, conv_dw_b, conv_ln_g, conv_ln_b,
           w_conv_out, lru_conv_w, lru_conv_b, lru_w_r, lru_b_r, lru_w_i, lru_b_i, lru_lambda, w_lru_out, w_mix_out,
           ffn_w1, ffn_w3, ffn_w2, moe_router_w, moe_router_b, moe_w1, moe_w3, moe_w2):
    front = jnp.zeros((B, PAD, D), F32)
    meta = jnp.broadcast_to(meta_tokens[None], (B, N_META, D))
    x = jnp.concatenate([jnp.concatenate([front, meta, x_prompt], axis=1).reshape(MP, D),
                         x_sample.reshape(MS, D)], axis=0)

    ck = cache_attn_k.reshape(DEPTH * DB, WIN, KVW)
    cv = cache_attn_v.reshape(DEPTH * DB, WIN, KVW)
    sc = state_conv.reshape(DEPTH * DB, CK - 1, CW)
    slc = state_lru_conv.reshape(DEPTH * DB, LCK - 1, LW)
    slh = state_lru_h.reshape(DEPTH * DB, LW)

    s_k = jnp.zeros((DEPTH * DB, WIN, KVW), F32)
    s_v = jnp.zeros((DEPTH * DB, WIN, KVW), F32)
    s_conv = jnp.zeros((DEPTH * DB, CK - 1, CW), F32)
    s_lconv = jnp.zeros((DEPTH * DB, LCK - 1, LW), F32)
    s_h = jnp.zeros((DEPTH * DB, LW), F32)
    p_k, p_v, p_conv, p_lconv, p_h = [], [], [], [], []

    assert DEPTH == 2, "layer 0 dense SwiGLU, layer 1 routed experts followed by the final norm"
    w_in_b = w_in.astype(BF16)
    wa, wc, wl, wm = (w.astype(BF16) for w in (w_attn_out, w_conv_out, w_lru_out, w_mix_out))
    ffn_b = tuple(w.astype(BF16) for w in (ffn_w1, ffn_w3, ffn_w2))
    moe_b = tuple(w.astype(BF16) for w in (moe_w1, moe_w3, moe_w2))

    xn = rmsnorm_cast(x, norm_mix[0])
    y_prompt = y_sample = None
    for l in range(DEPTH):
        z = inproj(xn, w_in_b, l)
        sinks = attn_sinks[l]

        o_p = attn_prompt(z, sinks)
        o_s, s_k, s_v = attn_sample(z, sinks, ck, cv, s_k, s_v, l)
        o = (o_p, o_s)
        kv_last = jnp.stack([lax.slice(z, ((b + 1) * PB - WIN, Z_K), ((b + 1) * PB, Z_K + 2 * KVW))
                             for b in range(B)]).astype(F32)
        p_k.append(kv_last[:, :, :KVW].reshape(B, WIN, NKV, HD))
        p_v.append(kv_last[:, :, KVW:].reshape(B, WIN, NKV, HD))

        c_p, pc = conv_prompt(z, conv_dw_w[l], conv_dw_b[l], conv_ln_g[l], conv_ln_b[l])
        c_s, s_conv = conv_sample(z, sc, conv_dw_w[l], conv_dw_b[l], conv_ln_g[l], conv_ln_b[l], s_conv, l)
        c = (c_p, c_s)
        p_conv.append(pc)

        wr = _blockdiag(lru_w_r[l])
        wi = _blockdiag(lru_w_i[l])
        lru_args = (lru_conv_w[l], lru_conv_b[l], wr, lru_b_r[l], wi, lru_b_i[l], lru_lambda[l])
        rr_p, plc, ph = lru_prompt(z, *lru_args)
        rr_s, s_lconv, s_h = lru_sample(z, slc, slh, *lru_args, s_lconv, s_h, l)
        rr = (rr_p, rr_s)
        p_lconv.append(plc)
        p_h.append(ph.reshape(B, LW))

        j = l // 2
        if l % 2 == 0:
            x1, hn = mix(o, c, rr, z, x, wa, wc, wl, wm, norm_ffn[l], l)
            h = ffn_up(hn, ffn_b[0], ffn_b[1], j)
            x, xn = ffn_down(h, ffn_b[2], x1, norm_mix[l + 1], j)
        else:
            rw = jnp.pad(moe_router_w[j], ((0, 0), (0, ROUTE_W - NE)))
            rw_hi = rw.astype(BF16)
            rw_lo = (rw - rw_hi.astype(F32)).astype(BF16)
            rb = jnp.pad(moe_router_b[j].reshape(1, NE), ((0, 0), (0, ROUTE_W - NE)))
            x1, route = mix(o, c, rr, z, x, wa, wc, wl, wm, norm_ffn[l], l,
                            router=(jnp.concatenate([rw_hi, rw_lo], axis=1), rb))
            slot1, slot2, tile_expert, n_used, zero_tiles = _moe_plan(route)
            xs = moe_scatter(x1, norm_ffn[l], slot1, slot2, zero_tiles)
            hs = moe_up(xs, moe_b[0], moe_b[1], tile_expert, n_used, j)
            ys = moe_down(hs, moe_b[2], tile_expert, n_used, j)
            y_prompt, y_sample = moe_combine(x1, route, norm_final, ys, slot1, slot2)

    y_prompt = y_prompt.reshape(B, SEQ, D)
    y_sample = y_sample.reshape(DB, T, D)
    st4 = lambda a: a.reshape(DEPTH, DB, WIN, NKV, HD)
    return (y_prompt, y_sample,
            jnp.stack(p_k), jnp.stack(p_v), jnp.stack(p_conv), jnp.stack(p_lconv), jnp.stack(p_h),
            st4(s_k), st4(s_v), s_conv.reshape(DEPTH, DB, CK - 1, CW), s_lconv.reshape(DEPTH, DB, LCK - 1, LW),
            s_h.reshape(DEPTH, DB, LW))
```

```python
import functools
import math

import jax
import jax.numpy as jnp
from jax import lax
from jax.experimental import pallas as pl
from jax.experimental.pallas import tpu as pltpu

F32 = jnp.float32
BF16 = jnp.bfloat16

D = 2048
B = 2
SEQ = 4096
DEPTH = 2
DB = 128
T = 8
PAST = 8192
N_META = 16
NH = 16
NKV = 4
G = NH // NKV
HD = 64
AW = NH * HD
KVW = NKV * HD
WIN = 128
BLK = 128
CW = D // 2
CK = 31
LW = D // 2
LCK = 4
LRU_C = 8.0
DFF = 5632
NE = 8
EPS = 1e-6
NEG = -1e30
IN_W = AW + 2 * KVW + 2 * CW + 2 * LW + 3 * D

PAD = (-N_META) % BLK
PB = PAD + N_META + SEQ
NB = PB // BLK
MP = B * PB
MS = DB * T
M = MP + MS

ZC = 512
NZC = IN_W // ZC
Z_Q, Z_A, Z_GATE, Z_LX, Z_LG, Z_GA, Z_K, Z_V = 0, 1024, 2048, 3072, 4096, 5120, 11264, 11520

SLOPES = tuple(2.0 ** (-8.0 * (h + 1.0) / NH) for h in range(NH))

TM_BIG = 1184
TM_MID = 592
TM_TOK = 256
SB = 8
MOE_TM = 256
MOE_TILES = (2 * M) // MOE_TM + NE
MOE_ROWS = MOE_TILES * MOE_TM
UP_TN = 1408
DOWN_TN = 1024

VMEM_LIMIT = 56 * 1024 * 1024


def _cp(*sem):
    return pltpu.CompilerParams(dimension_semantics=sem, vmem_limit_bytes=VMEM_LIMIT)


def _rms(x, g):
    return x * lax.rsqrt(jnp.mean(x * x, axis=-1, keepdims=True) + EPS) * g


def _sigmoid(x):
    return 1.0 / (1.0 + jnp.exp(-x))


def _silu(x):
    return x * _sigmoid(x)


def _prompt_block(i):
    ic = jnp.minimum(i, B * NB - 1)
    return (ic // NB) * (NB - 1) + jnp.maximum(ic % NB - 1, 0)


def _embed_norm_kernel(xp_ref, xs_ref, meta_ref, g_ref, x_ref, xn_ref):
    i = pl.program_id(0)
    head = jnp.concatenate([jnp.zeros((PAD, D), F32), meta_ref[...]], axis=0)
    x = jnp.where(i % NB == 0, head, xp_ref[...])
    x = jnp.where(i < B * NB, x, xs_ref[...])
    x_ref[...] = x
    xn_ref[...] = _rms(x, g_ref[...]).astype(xn_ref.dtype)


def embed_norm(x_prompt, x_sample, meta, g):
    tm = BLK
    return pl.pallas_call(
        _embed_norm_kernel,
        grid=(M // tm,),
        in_specs=[pl.BlockSpec((tm, D), lambda i: (_prompt_block(i), 0)),
                  pl.BlockSpec((tm, D), lambda i: (jnp.maximum(i - B * NB, 0), 0)),
                  pl.BlockSpec((N_META, D), lambda i: (0, 0)), pl.BlockSpec((1, D), lambda i: (0, 0))],
        out_specs=[pl.BlockSpec((tm, D), lambda i: (i, 0)), pl.BlockSpec((tm, D), lambda i: (i, 0))],
        out_shape=[jax.ShapeDtypeStruct((M, D), F32), jax.ShapeDtypeStruct((M, D), BF16)],
        compiler_params=_cp("parallel"),
        name="embed_norm",
    )(x_prompt.reshape(B * SEQ, D), x_sample.reshape(MS, D), meta, g.reshape(1, D))


def _cast_side(src_ref, dst_ref, step, nsteps):
    @pl.when(step < nsteps)
    def _():
        dst_ref[...] = src_ref[...].astype(dst_ref.dtype)


def _side_specs(shape2d, rows, step_of):
    nblk = shape2d[0] // rows
    assert nblk * rows == shape2d[0]
    spec = pl.BlockSpec((rows, shape2d[1]), lambda *g: (jnp.minimum(step_of(*g), nblk - 1), 0))
    return nblk, spec


def _inproj_kernel(x_ref, w_ref, side_ref, o_ref, side_out_ref, *, side_steps):
    o_ref[...] = jnp.dot(x_ref[...], w_ref[...], preferred_element_type=F32).astype(o_ref.dtype)
    _cast_side(side_ref, side_out_ref, pl.program_id(0) * NZC + pl.program_id(1), side_steps)


def _z_src_block(j):
    return jnp.where(j < 2, j, jnp.where(j < NZC - 1, j + 1, 2))


SIDE_ROWS_INPROJ = 128


def inproj(xn, w, layer, side):
    tm = TM_BIG
    nblk, sspec = _side_specs(side.shape, SIDE_ROWS_INPROJ, lambda i, j: i * NZC + j)
    assert nblk <= (M // tm) * NZC
    return pl.pallas_call(
        functools.partial(_inproj_kernel, side_steps=nblk),
        grid=(M // tm, NZC),
        in_specs=[pl.BlockSpec((tm, D), lambda i, j: (i, 0)),
                  pl.BlockSpec((None, D, ZC), lambda i, j: (layer, 0, _z_src_block(j))), sspec],
        out_specs=[pl.BlockSpec((tm, ZC), lambda i, j: (i, j)), sspec],
        out_shape=[jax.ShapeDtypeStruct((M, IN_W), BF16), jax.ShapeDtypeStruct(side.shape, BF16)],
        compiler_params=_cp("arbitrary", "arbitrary"),
        name="inproj",
    )(xn, w, side)


def _softmax_sink(s, sink):
    m = jnp.maximum(jnp.max(s, axis=-1, keepdims=True), sink)
    e = jnp.exp(s - m)
    l = jnp.sum(e, axis=-1, keepdims=True) + jnp.exp(sink - m)
    return e / l


def _attn_prompt_kernel(sink_ref, q_ref, kp_ref, kc_ref, vp_ref, vc_ref, o_ref, bias_ref):
    n = pl.program_id(1)
    c = lax.broadcasted_iota(jnp.int32, (BLK, 2 * BLK), 1)

    @pl.when(n == 0)
    def _():
        r = lax.broadcasted_iota(jnp.int32, (BLK, 2 * BLK), 0)
        dist = BLK + r - c
        in_window = (dist >= 0) & (dist <= WIN)
        distf = dist.astype(F32)
        for hq in range(NH):
            bias_ref[hq] = jnp.where(in_window, -SLOPES[hq] * distf, NEG)

    exists = (n - 1) * BLK - PAD + c >= 0
    q = q_ref[...] * (HD ** -0.5)
    k = jnp.concatenate([kp_ref[...], kc_ref[...]], axis=0)
    v = jnp.concatenate([vp_ref[...], vc_ref[...]], axis=0)
    for h in range(NKV):
        kh = k[:, h * HD:(h + 1) * HD]
        vh = v[:, h * HD:(h + 1) * HD]
        qh = jnp.concatenate([q[:, (h * G + g) * HD:(h * G + g + 1) * HD] for g in range(G)], axis=0)
        s = lax.dot_general(qh, kh, (((1,), (1,)), ((), ())), preferred_element_type=F32)
        ps = []
        for g in range(G):
            hq = h * G + g
            sg = jnp.where(exists, s[g * BLK:(g + 1) * BLK] + bias_ref[hq], NEG)
            ps.append(_softmax_sink(sg, sink_ref[hq]).astype(BF16))
        oh = jnp.dot(jnp.concatenate(ps, axis=0), vh, preferred_element_type=F32)
        for g in range(G):
            hq = h * G + g
            o_ref[:, hq * HD:(hq + 1) * HD] = oh[g * BLK:(g + 1) * BLK].astype(o_ref.dtype)


def attn_prompt(z, sinks):
    kb, vb = Z_K // KVW, Z_V // KVW

    def cur(col):
        return lambda b, n: (b * NB + n, col)

    def prev(col):
        return lambda b, n: (b * NB + jnp.maximum(n - 1, 0), col)

    return pl.pallas_call(
        _attn_prompt_kernel,
        grid=(B, NB),
        in_specs=[pl.BlockSpec(memory_space=pltpu.SMEM),
                  pl.BlockSpec((BLK, AW), cur(0)),
                  pl.BlockSpec((BLK, KVW), prev(kb)), pl.BlockSpec((BLK, KVW), cur(kb)),
                  pl.BlockSpec((BLK, KVW), prev(vb)), pl.BlockSpec((BLK, KVW), cur(vb))],
        out_specs=pl.BlockSpec((BLK, AW), cur(0)),
        out_shape=jax.ShapeDtypeStruct((MP, AW), BF16),
        scratch_shapes=[pltpu.VMEM((NH, BLK, 2 * BLK), F32)],
        compiler_params=_cp("parallel", "arbitrary"),
        name="attn_prompt",
    )(sinks, z, z, z, z, z)


def _attn_sample_kernel(sink_ref, q_ref, kn_ref, vn_ref, ck_ref, cv_ref, nk_in_ref, nv_in_ref,
                        o_ref, nk_ref, nv_ref):
    del nk_in_ref, nv_in_ref
    q = q_ref[...].astype(F32).reshape(SB, T, AW)
    kn = kn_ref[...].astype(F32).reshape(SB, T, KVW)
    vn = vn_ref[...].astype(F32).reshape(SB, T, KVW)
    ck = ck_ref[...]
    cv = cv_ref[...]
    wb = WIN
    nk_ref[:, 0:wb - T, :] = ck[:, T:wb, :]
    nk_ref[:, wb - T:wb, :] = kn
    nv_ref[:, 0:wb - T, :] = cv[:, T:wb, :]
    nv_ref[:, wb - T:wb, :] = vn
    zpad = jnp.zeros((SB, wb - T, KVW), F32)
    k = jnp.concatenate([ck, kn, zpad], axis=1).astype(BF16)
    v = jnp.concatenate([cv, vn, zpad], axis=1).astype(BF16)
    nkeys = 2 * wb
    r = lax.broadcasted_iota(jnp.int32, (G * T, nkeys), 0)
    c = lax.broadcasted_iota(jnp.int32, (G * T, nkeys), 1)
    dist = wb + (r % T) - c
    allowed = (dist >= 0) & (dist <= WIN)
    distf = dist.astype(F32)
    gidx = r // T
    for h in range(NKV):
        slope = jnp.zeros((G * T, nkeys), F32)
        sink = jnp.zeros((G * T, 1), F32)
        for g in range(G):
            slope = jnp.where(gidx == g, SLOPES[h * G + g], slope)
            sink = jnp.where(gidx[:, 0:1] == g, sink_ref[h * G + g], sink)
        kh = k[:, :, h * HD:(h + 1) * HD]
        vh = v[:, :, h * HD:(h + 1) * HD]
        qh = jnp.concatenate([q[:, :, (h * G + g) * HD:(h * G + g + 1) * HD] for g in range(G)],
                             axis=1).astype(BF16)
        s = jnp.einsum("bqd,bkd->bqk", qh, kh, preferred_element_type=F32)
        s = s * (HD ** -0.5) - (slope * distf)[None]
        s = jnp.where(allowed[None], s, NEG)
        p = _softmax_sink(s, sink[None]).astype(BF16)
        oh = jnp.einsum("bqk,bkd->bqd", p, vh, preferred_element_type=F32)
        for g in range(G):
            hq = h * G + g
            piece = oh[:, g * T:(g + 1) * T, :].reshape(SB * T, HD)
            o_ref[:, hq * HD:(hq + 1) * HD] = piece.astype(o_ref.dtype)


def attn_sample(z, sinks, cache_k, cache_v, nk, nv, layer):
    rows = SB * T
    rb0 = MP // rows
    kb, vb = Z_K // KVW, Z_V // KVW
    cb0 = layer * (DB // SB)
    cspec = pl.BlockSpec((SB, WIN, KVW), lambda i: (cb0 + i, 0, 0))
    anyspec = pl.BlockSpec(memory_space=pl.ANY)
    return pl.pallas_call(
        _attn_sample_kernel,
        grid=(DB // SB,),
        in_specs=[pl.BlockSpec(memory_space=pltpu.SMEM),
                  pl.BlockSpec((rows, AW), lambda i: (rb0 + i, 0)),
                  pl.BlockSpec((rows, KVW), lambda i: (rb0 + i, kb)),
                  pl.BlockSpec((rows, KVW), lambda i: (rb0 + i, vb)),
                  cspec, cspec, anyspec, anyspec],
        out_specs=[pl.BlockSpec((rows, AW), lambda i: (i, 0)), cspec, cspec],
        out_shape=[jax.ShapeDtypeStruct((MS, AW), BF16),
                   jax.ShapeDtypeStruct((DEPTH * DB, WIN, KVW), F32),
                   jax.ShapeDtypeStruct((DEPTH * DB, WIN, KVW), F32)],
        input_output_aliases={6: 1, 7: 2},
        compiler_params=_cp("parallel"),
        name="attn_sample",
    )(sinks, z, z, z, cache_k, cache_v, nk, nv)


def _ln_swish(y, g, b):
    mu = jnp.mean(y, axis=-1, keepdims=True)
    yc = y - mu
    var = jnp.mean(yc * yc, axis=-1, keepdims=True)
    yn = yc * lax.rsqrt(var + EPS) * g + b
    return _silu(yn)


CONV_HIST = 32
SUBLANES = 8


def _dwconv_by_phase(load, wrow, nrows, first, ntaps):
    y = None
    for s in range(SUBLANES):
        n = nrows if s == 0 else nrows + SUBLANES
        part = None
        for q in range((first + ntaps - 1) // SUBLANES + 1):
            o = SUBLANES * q + s
            if first <= o < first + ntaps:
                term = wrow(o - first) * load(SUBLANES * q, n)
                part = term if part is None else part + term
        if part is not None:
            part = part[s:s + nrows]
            y = part if y is None else y + part
    return y


def _conv_prompt_kernel(a_ref, g_ref, w_ref, b_ref, lg_ref, lb_ref, c_ref, st_ref, ext_ref, y_ref):
    n = pl.program_id(1)

    @pl.when(n == 0)
    def _():
        ext_ref[0:CONV_HIST, :] = jnp.zeros((CONV_HIST, CW), F32)

    @pl.when(n > 0)
    def _():
        ext_ref[0:CONV_HIST, :] = ext_ref[BLK:BLK + CONV_HIST, :]

    u = a_ref[...].astype(F32) * _sigmoid(g_ref[...].astype(F32))
    row = lax.broadcasted_iota(jnp.int32, (BLK, 1), 0)
    u = jnp.where((n > 0) | (row >= PAD), u, 0.0)
    ext_ref[CONV_HIST:CONV_HIST + BLK, :] = u
    off = CONV_HIST - (CK - 1)
    cc = 256
    for c0 in range(0, CW, cc):
        cs = slice(c0, c0 + cc)
        y_ref[:, cs] = b_ref[:, cs] + _dwconv_by_phase(lambda r0, nr: ext_ref[r0:r0 + nr, cs],
                                                      lambda j: w_ref[j:j + 1, cs], BLK, off, CK)
    c_ref[...] = _ln_swish(y_ref[...], lg_ref[...], lb_ref[...]).astype(c_ref.dtype)

    @pl.when(n == NB - 1)
    def _():
        st_ref[0] = ext_ref[CONV_HIST + BLK - (CK - 1):CONV_HIST + BLK, :]


def conv_prompt(z, w, b, lg, lb):
    vec = pl.BlockSpec((1, CW), lambda bb, n: (0, 0))
    return pl.pallas_call(
        _conv_prompt_kernel,
        grid=(B, NB),
        in_specs=[pl.BlockSpec((BLK, CW), lambda bb, n: (bb * NB + n, Z_A // CW)),
                  pl.BlockSpec((BLK, CW), lambda bb, n: (bb * NB + n, Z_GATE // CW)),
                  pl.BlockSpec((CK, CW), lambda bb, n: (0, 0)), vec, vec, vec],
        out_specs=[pl.BlockSpec((BLK, CW), lambda bb, n: (bb * NB + n, 0)),
                   pl.BlockSpec((1, CK - 1, CW), lambda bb, n: (bb, 0, 0))],
        out_shape=[jax.ShapeDtypeStruct((MP, CW), BF16), jax.ShapeDtypeStruct((B, CK - 1, CW), F32)],
        scratch_shapes=[pltpu.VMEM((CONV_HIST + BLK, CW), F32), pltpu.VMEM((BLK, CW), F32)],
        compiler_params=_cp("parallel", "arbitrary"),
        name="conv_prompt",
    )(z, z, w, b.reshape(1, CW), lg.reshape(1, CW), lb.reshape(1, CW))


def _conv_sample_kernel(a_ref, g_ref, st_ref, w_ref, b_ref, lg_ref, lb_ref, nst_in_ref,
                        c_ref, nst_ref, ext_ref, y_ref):
    del nst_in_ref
    u = a_ref[...].astype(F32) * _sigmoid(g_ref[...].astype(F32))
    hist = CK - 1
    base = CONV_HIST - hist
    for s in range(SB):
        ext_ref[s, 0:base, :] = jnp.zeros((base, CW), F32)
        ext_ref[s, base:CONV_HIST, :] = st_ref[s]
        ext_ref[s, CONV_HIST:CONV_HIST + T, :] = u[s * T:(s + 1) * T]
    for s in range(SB):
        y_ref[s * T:(s + 1) * T, :] = b_ref[...] + _dwconv_by_phase(
            lambda r0, nr: ext_ref[s, r0:r0 + nr, :], lambda j: w_ref[j:j + 1, :], T, base, CK)
        nst_ref[s] = ext_ref[s, base + T:base + T + hist, :]
    c_ref[...] = _ln_swish(y_ref[...], lg_ref[...], lb_ref[...]).astype(c_ref.dtype)


def conv_sample(z, state, w, b, lg, lb, nst, layer):
    rows = SB * T
    rb0 = MP // rows
    sb0 = layer * (DB // SB)
    vec = pl.BlockSpec((1, CW), lambda i: (0, 0))
    sspec = pl.BlockSpec((SB, CK - 1, CW), lambda i: (sb0 + i, 0, 0))
    anyspec = pl.BlockSpec(memory_space=pl.ANY)
    return pl.pallas_call(
        _conv_sample_kernel,
        grid=(DB // SB,),
        in_specs=[pl.BlockSpec((rows, CW), lambda i: (rb0 + i, Z_A // CW)),
                  pl.BlockSpec((rows, CW), lambda i: (rb0 + i, Z_GATE // CW)),
                  sspec, pl.BlockSpec((CK, CW), lambda i: (0, 0)), vec, vec, vec, anyspec],
        out_specs=[pl.BlockSpec((rows, CW), lambda i: (i, 0)), sspec],
        out_shape=[jax.ShapeDtypeStruct((MS, CW), BF16), jax.ShapeDtypeStruct((DEPTH * DB, CK - 1, CW), F32)],
        scratch_shapes=[pltpu.VMEM((SB, CONV_HIST + T, CW), F32), pltpu.VMEM((rows, CW), F32)],
        input_output_aliases={7: 1},
        compiler_params=_cp("parallel"),
        name="conv_sample",
    )(z, z, state, w, b.reshape(1, CW), lg.reshape(1, CW), lb.reshape(1, CW), nst)


LRU_HIST = 8
GATE_CH = 256


def _gelu_tanh(x):
    return x * (0.5 * (1.0 + jnp.tanh(math.sqrt(2.0 / math.pi) * (x + 0.044715 * (x * x * x)))))


def _softplus(x):
    return jnp.maximum(x, 0.0) + jnp.log1p(jnp.exp(-jnp.abs(x)))


def _expm1(x):
    return jnp.tanh(0.5 * x) * (jnp.exp(x) + 1.0)


def _lru_gates(xc, wr_ref, br_ref, wi_ref, bi_ref, lam_ref, valid, a_ref, bx_ref):
    sp = _softplus(-lam_ref[...])
    for k in range(LW // GATE_CH):
        sl = slice(k * GATE_CH, (k + 1) * GATE_CH)
        xk = xc[:, sl]
        xkb = xk.astype(BF16)
        r = _sigmoid(jnp.dot(xkb, wr_ref[k], preferred_element_type=F32) + br_ref[:, sl])
        i = _sigmoid(jnp.dot(xkb, wi_ref[k], preferred_element_type=F32) + bi_ref[:, sl])
        log_a = (-LRU_C) * r * sp[:, sl]
        a = jnp.exp(log_a)
        bx = jnp.sqrt(-_expm1(2.0 * log_a)) * (i * xk)
        if valid is not None:
            bx = jnp.where(valid, bx, 0.0)
        a_ref[:, sl] = a
        bx_ref[:, sl] = bx


def _lru_prompt_kernel(x_ref, g_ref, cw_ref, cb_ref, wr_ref, br_ref, wi_ref, bi_ref, lam_ref,
                       rr_ref, cst_ref, hst_ref, ext_ref, a_ref, bx_ref, h_ref, carry_ref):
    n = pl.program_id(1)

    @pl.when(n == 0)
    def _():
        ext_ref[0:LRU_HIST, :] = jnp.zeros((LRU_HIST, LW), F32)
        carry_ref[...] = jnp.zeros((1, LW), F32)

    @pl.when(n > 0)
    def _():
        ext_ref[0:LRU_HIST, :] = ext_ref[BLK:BLK + LRU_HIST, :]

    row = lax.broadcasted_iota(jnp.int32, (BLK, 1), 0)
    valid = (n > 0) | (row >= PAD)
    ext_ref[LRU_HIST:LRU_HIST + BLK, :] = jnp.where(valid, x_ref[...].astype(F32), 0.0)
    off = LRU_HIST - (LCK - 1)
    xc = jnp.zeros((BLK, LW), F32) + cb_ref[...]
    for j in range(LCK):
        xc = xc + cw_ref[j:j + 1, :] * ext_ref[off + j:off + j + BLK, :]
    _lru_gates(xc, wr_ref, br_ref, wi_ref, bi_ref, lam_ref, valid, a_ref, bx_ref)

    def step(t, h):
        h = a_ref[pl.ds(t, 1), :] * h + bx_ref[pl.ds(t, 1), :]
        h_ref[pl.ds(t, 1), :] = h
        return h

    h = lax.fori_loop(0, BLK, step, carry_ref[...], unroll=8)
    carry_ref[...] = h
    rr_ref[...] = (h_ref[...] * _gelu_tanh(g_ref[...].astype(F32))).astype(rr_ref.dtype)

    @pl.when(n == NB - 1)
    def _():
        cst_ref[0] = ext_ref[LRU_HIST + BLK - (LCK - 1):LRU_HIST + BLK, :]
        hst_ref[0] = h


def lru_prompt(z, cw, cb, wr, br, wi, bi, lam):
    vec = pl.BlockSpec((1, LW), lambda bb, n: (0, 0))
    wspec = pl.BlockSpec((LW // GATE_CH, GATE_CH, GATE_CH), lambda bb, n: (0, 0, 0))
    return pl.pallas_call(
        _lru_prompt_kernel,
        grid=(B, NB),
        in_specs=[pl.BlockSpec((BLK, LW), lambda bb, n: (bb * NB + n, Z_LX // LW)),
                  pl.BlockSpec((BLK, LW), lambda bb, n: (bb * NB + n, Z_LG // LW)),
                  pl.BlockSpec((LCK, LW), lambda bb, n: (0, 0)), vec, wspec, vec, wspec, vec, vec],
        out_specs=[pl.BlockSpec((BLK, LW), lambda bb, n: (bb * NB + n, 0)),
                   pl.BlockSpec((1, LCK - 1, LW), lambda bb, n: (bb, 0, 0)),
                   pl.BlockSpec((1, 1, LW), lambda bb, n: (bb, 0, 0))],
        out_shape=[jax.ShapeDtypeStruct((MP, LW), BF16), jax.ShapeDtypeStruct((B, LCK - 1, LW), F32),
                   jax.ShapeDtypeStruct((B, 1, LW), F32)],
        scratch_shapes=[pltpu.VMEM((LRU_HIST + BLK, LW), F32), pltpu.VMEM((BLK, LW), F32),
                        pltpu.VMEM((BLK, LW), F32), pltpu.VMEM((BLK, LW), F32), pltpu.VMEM((1, LW), F32)],
        compiler_params=_cp("parallel", "arbitrary"),
        name="lru_prompt",
    )(z, z, cw, cb.reshape(1, LW), wr, br.reshape(1, LW), wi, bi.reshape(1, LW), lam.reshape(1, LW))


def _lru_sample_kernel(x_ref, g_ref, cst_ref, h0_ref, cw_ref, cb_ref, wr_ref, br_ref, wi_ref, bi_ref, lam_ref,
                       ncst_in_ref, nh_in_ref, rr_ref, ncst_ref, nh_ref,
                       ext_ref, xc_ref, a_ref, bx_ref, h_ref):
    del ncst_in_ref, nh_in_ref
    hist = LCK - 1
    base = LRU_HIST - hist
    x = x_ref[...].astype(F32)
    for s in range(SB):
        ext_ref[s, base:LRU_HIST, :] = cst_ref[s]
        ext_ref[s, LRU_HIST:LRU_HIST + T, :] = x[s * T:(s + 1) * T]
    for s in range(SB):
        acc = jnp.zeros((T, LW), F32) + cb_ref[...]
        for j in range(LCK):
            acc = acc + cw_ref[j:j + 1, :] * ext_ref[s, base + j:base + j + T, :]
        xc_ref[s * T:(s + 1) * T, :] = acc
        ncst_ref[s] = ext_ref[s, base + T:base + T + hist, :]
    _lru_gates(xc_ref[...], wr_ref, br_ref, wi_ref, bi_ref, lam_ref, None, a_ref, bx_ref)
    for s in range(SB):
        h = h0_ref[s:s + 1, :]
        for t in range(T):
            rw = s * T + t
            h = a_ref[rw:rw + 1, :] * h + bx_ref[rw:rw + 1, :]
            h_ref[rw:rw + 1, :] = h
        nh_ref[s:s + 1, :] = h
    rr_ref[...] = (h_ref[...] * _gelu_tanh(g_ref[...].astype(F32))).astype(rr_ref.dtype)


def lru_sample(z, cstate, hstate, cw, cb, wr, br, wi, bi, lam, ncst, nh, layer):
    rows = SB * T
    rb0 = MP // rows
    sb0 = layer * (DB // SB)
    vec = pl.BlockSpec((1, LW), lambda i: (0, 0))
    wspec = pl.BlockSpec((LW // GATE_CH, GATE_CH, GATE_CH), lambda i: (0, 0, 0))
    cspec = pl.BlockSpec((SB, LCK - 1, LW), lambda i: (sb0 + i, 0, 0))
    hspec = pl.BlockSpec((SB, LW), lambda i: (sb0 + i, 0))
    anyspec = pl.BlockSpec(memory_space=pl.ANY)
    return pl.pallas_call(
        _lru_sample_kernel,
        grid=(DB // SB,),
        in_specs=[pl.BlockSpec((rows, LW), lambda i: (rb0 + i, Z_LX // LW)),
                  pl.BlockSpec((rows, LW), lambda i: (rb0 + i, Z_LG // LW)),
                  cspec, hspec, pl.BlockSpec((LCK, LW), lambda i: (0, 0)), vec, wspec, vec, wspec, vec, vec,
                  anyspec, anyspec],
        out_specs=[pl.BlockSpec((rows, LW), lambda i: (i, 0)), cspec, hspec],
        out_shape=[jax.ShapeDtypeStruct((MS, LW), BF16), jax.ShapeDtypeStruct((DEPTH * DB, LCK - 1, LW), F32),
                   jax.ShapeDtypeStruct((DEPTH * DB, LW), F32)],
        scratch_shapes=[pltpu.VMEM((SB, LRU_HIST + T, LW), F32)] + [pltpu.VMEM((rows, LW), F32)] * 4,
        input_output_aliases={11: 1, 12: 2},
        compiler_params=_cp("parallel"),
        name="lru_sample",
    )(z, z, cstate, hstate, cw, cb.reshape(1, LW), wr, br.reshape(1, LW), wi, bi.reshape(1, LW),
      lam.reshape(1, LW), ncst, nh)


ROUTE_W = 128


def _mix_body(branch_refs, gates, x_ref, wa_ref, wc_ref, wl_ref, wm_ref, g_ref, x1_ref):
    is_prompt = pl.program_id(0) < MP // TM_TOK
    o, c, r = (jnp.where(is_prompt, p_ref[...], s_ref[...]) for p_ref, s_ref in branch_refs)
    half = D // 2
    parts = []
    for hh in range(2):
        sl = slice(hh * half, (hh + 1) * half)
        m = _sigmoid(gates[0][hh][...].astype(F32)) * jnp.dot(o, wa_ref[:, sl], preferred_element_type=F32)
        m = m + _sigmoid(gates[1][hh][...].astype(F32)) * jnp.dot(c, wc_ref[:, sl], preferred_element_type=F32)
        m = m + _sigmoid(gates[2][hh][...].astype(F32)) * jnp.dot(r, wl_ref[:, sl], preferred_element_type=F32)
        parts.append(m.astype(BF16))
    merged = jnp.concatenate(parts, axis=1)
    x1 = x_ref[...] + jnp.dot(merged, wm_ref[...], preferred_element_type=F32)
    x1_ref[...] = x1
    return _rms(x1, g_ref[...])


def _mix_kernel(op_ref, os_ref, cp_ref, cs_ref, rp_ref, rs_ref, ga0, ga1, gb0, gb1, gc0, gc1, x_ref,
                wa_ref, wc_ref, wl_ref, wm_ref, g_ref, x1_ref, hn_ref):
    hn = _mix_body(((op_ref, os_ref), (cp_ref, cs_ref), (rp_ref, rs_ref)), ((ga0, ga1), (gb0, gb1), (gc0, gc1)),
                   x_ref, wa_ref, wc_ref, wl_ref, wm_ref, g_ref, x1_ref)
    hn_ref[...] = hn.astype(hn_ref.dtype)


def _mix_router_kernel(op_ref, os_ref, cp_ref, cs_ref, rp_ref, rs_ref, ga0, ga1, gb0, gb1, gc0, gc1, x_ref,
                       wa_ref, wc_ref, wl_ref, wm_ref, g_ref, rw_ref, rb_ref, x1_ref, route_ref):
    hn = _mix_body(((op_ref, os_ref), (cp_ref, cs_ref), (rp_ref, rs_ref)), ((ga0, ga1), (gb0, gb1), (gc0, gc1)),
                   x_ref, wa_ref, wc_ref, wl_ref, wm_ref, g_ref, x1_ref)
    h_hi = hn.astype(BF16)
    h_lo = (hn - h_hi.astype(F32)).astype(BF16)
    both = jnp.dot(h_hi, rw_ref[...], preferred_element_type=F32)
    logits = (both[:, :ROUTE_W] + both[:, ROUTE_W:]
              + jnp.dot(h_lo, rw_ref[:, :ROUTE_W], preferred_element_type=F32) + rb_ref[...])
    lane = lax.broadcasted_iota(jnp.int32, logits.shape, 1)
    ninf = -jnp.inf
    l1 = jnp.where(lane < NE, logits, ninf)
    m1 = jnp.max(l1, axis=-1, keepdims=True)
    i1 = jnp.min(jnp.where(l1 == m1, lane, ROUTE_W), axis=-1, keepdims=True)
    l2 = jnp.where(lane == i1, ninf, l1)
    m2 = jnp.max(l2, axis=-1, keepdims=True)
    i2 = jnp.min(jnp.where(l2 == m2, lane, ROUTE_W), axis=-1, keepdims=True)
    e2 = jnp.exp(m2 - m1)
    den = 1.0 + e2
    out = jnp.where(lane == 0, i1.astype(F32), 0.0)
    out = jnp.where(lane == 1, i2.astype(F32), out)
    out = jnp.where(lane == 2, 1.0 / den, out)
    out = jnp.where(lane == 3, e2 / den, out)
    route_ref[...] = out


def mix(o, c, rr, z, x, wa, wc, wl, wm, g, layer, router=None):
    tm = TM_TOK
    half = D // 2
    npt = MP // tm
    row = lambda col: (lambda i: (i, col))
    const = lambda shape: pl.BlockSpec(shape, lambda i: (0,) * len(shape), pipeline_mode=pl.Buffered(1))
    wspec = lambda k: pl.BlockSpec((None, k, D), lambda i: (layer, 0, 0), pipeline_mode=pl.Buffered(1))
    gate_specs = [pl.BlockSpec((tm, half), row(Z_GA // half + k)) for k in range(6)]
    pair = lambda w: [pl.BlockSpec((tm, w), lambda i: (jnp.minimum(i, npt - 1), 0)),
                      pl.BlockSpec((tm, w), lambda i: (jnp.maximum(i - npt, 0), 0))]
    in_specs = (pair(AW) + pair(CW) + pair(LW) + gate_specs
                + [pl.BlockSpec((tm, D), row(0)), wspec(AW), wspec(CW), wspec(LW), wspec(D), const((1, D))])
    args = [*o, *c, *rr, z, z, z, z, z, z, x, wa, wc, wl, wm, g.reshape(1, D)]
    if router is None:
        kern = _mix_kernel
        out_specs = [pl.BlockSpec((tm, D), row(0)), pl.BlockSpec((tm, D), row(0))]
        out_shape = [jax.ShapeDtypeStruct((M, D), F32), jax.ShapeDtypeStruct((M, D), BF16)]
    else:
        kern = _mix_router_kernel
        rw, rb = router
        in_specs += [const((D, 2 * ROUTE_W)), const((1, ROUTE_W))]
        args += [rw, rb]
        out_specs = [pl.BlockSpec((tm, D), row(0)), pl.BlockSpec((tm, ROUTE_W), row(0))]
        out_shape = [jax.ShapeDtypeStruct((M, D), F32), jax.ShapeDtypeStruct((M, ROUTE_W), F32)]
    return pl.pallas_call(
        kern,
        grid=(M // tm,),
        in_specs=in_specs,
        out_specs=out_specs,
        out_shape=out_shape,
        compiler_params=_cp("parallel"),
        name="mix",
    )(*args)


def _ffn_up_kernel(x_ref, w1_ref, w3_ref, side_ref, o_ref, side_out_ref, *, side_steps):
    x = x_ref[...]
    a = jnp.dot(x, w1_ref[...], preferred_element_type=F32)
    b = jnp.dot(x, w3_ref[...], preferred_element_type=F32)
    o_ref[...] = (_silu(a) * b).astype(o_ref.dtype)
    _cast_side(side_ref, side_out_ref, pl.program_id(0) * pl.num_programs(1) + pl.program_id(1), side_steps)


SIDE_ROWS_FFN = 512


def ffn_up(hn, w1, w3, idx, side):
    tm, tn = TM_BIG, 512
    nj = DFF // tn
    wspec = pl.BlockSpec((None, D, tn), lambda i, j: (idx, 0, j))
    nblk, sspec = _side_specs(side.shape, SIDE_ROWS_FFN, lambda i, j: i * nj + j)
    assert nblk <= (M // tm) * nj
    return pl.pallas_call(
        functools.partial(_ffn_up_kernel, side_steps=nblk),
        grid=(M // tm, nj),
        in_specs=[pl.BlockSpec((tm, D), lambda i, j: (i, 0)), wspec, wspec, sspec],
        out_specs=[pl.BlockSpec((tm, tn), lambda i, j: (i, j)), sspec],
        out_shape=[jax.ShapeDtypeStruct((M, DFF), BF16), jax.ShapeDtypeStruct(side.shape, BF16)],
        compiler_params=_cp("arbitrary", "arbitrary"),
        name="ffn_up",
    )(hn, w1, w3, side)


def _ffn_down_kernel(h_ref, w_ref, x_ref, g_ref, x2_ref, xn_ref):
    x2 = x_ref[...] + jnp.dot(h_ref[...], w_ref[...], preferred_element_type=F32)
    x2_ref[...] = x2
    xn_ref[...] = _rms(x2, g_ref[...]).astype(xn_ref.dtype)


def ffn_down(h, w2, x1, g, idx):
    tm = TM_TOK
    return pl.pallas_call(
        _ffn_down_kernel,
        grid=(M // tm,),
        in_specs=[pl.BlockSpec((tm, DFF), lambda i: (i, 0)),
                  pl.BlockSpec((None, DFF, D), lambda i: (idx, 0, 0), pipeline_mode=pl.Buffered(1)),
                  pl.BlockSpec((tm, D), lambda i: (i, 0)), pl.BlockSpec((1, D), lambda i: (0, 0))],
        out_specs=[pl.BlockSpec((tm, D), lambda i: (i, 0)), pl.BlockSpec((tm, D), lambda i: (i, 0))],
        out_shape=[jax.ShapeDtypeStruct((M, D), F32), jax.ShapeDtypeStruct((M, D), BF16)],
        compiler_params=_cp("parallel"),
        name="ffn_down",
    )(h, w2, x1, g.reshape(1, D))


def _row_copy(src, src_row, dst, dst_row, sem):
    return pltpu.make_async_copy(src.at[pl.ds(src_row, 1)], dst.at[pl.ds(dst_row, 1)], sem)


def _moe_scatter_kernel(s1_ref, s2_ref, zt_ref, x_ref, g_ref, xs_ref, buf_ref, zero_ref, sem):
    i = pl.program_id(0)
    tm = MOE_TM

    def tile_copy(e):
        return pltpu.make_async_copy(zero_ref, xs_ref.at[pl.ds(pl.multiple_of(zt_ref[e] * tm, tm), tm)], sem)

    @pl.when(i == 0)
    def _():
        zero_ref[...] = jnp.zeros(zero_ref.shape, F32)
        for e in range(2 * NE):
            tile_copy(e).start()
            tile_copy(e).wait()

    buf_ref[...] = _rms(x_ref[...], g_ref[...])

    def issue(r, carry):
        t = i * tm + r
        _row_copy(buf_ref, r, xs_ref, s1_ref[t], sem).start()
        _row_copy(buf_ref, r, xs_ref, s2_ref[t], sem).start()
        return carry

    lax.fori_loop(0, tm, issue, 0, unroll=8)

    def drain(r, carry):
        _row_copy(buf_ref, 0, xs_ref, 0, sem).wait()
        _row_copy(buf_ref, 0, xs_ref, 0, sem).wait()
        return carry

    lax.fori_loop(0, tm, drain, 0, unroll=8)


def moe_scatter(x1, g, slot1, slot2, zero_tiles):
    tm = MOE_TM
    return pl.pallas_call(
        _moe_scatter_kernel,
        grid_spec=pltpu.PrefetchScalarGridSpec(
            num_scalar_prefetch=3,
            grid=(M // tm,),
            in_specs=[pl.BlockSpec((tm, D), lambda i, *_: (i, 0)), pl.BlockSpec((1, D), lambda i, *_: (0, 0))],
            out_specs=pl.BlockSpec(memory_space=pl.ANY),
            scratch_shapes=[pltpu.VMEM((tm, D), F32), pltpu.VMEM((tm, D), F32), pltpu.SemaphoreType.DMA(())]),
        out_shape=jax.ShapeDtypeStruct((MOE_ROWS, D), F32),
        compiler_params=_cp("arbitrary"),
        name="moe_scatter",
    )(slot1, slot2, zero_tiles, x1, g.reshape(1, D))


def _moe_up_kernel(te_ref, nu_ref, x_ref, w1_ref, w3_ref, o_ref):
    used = pl.program_id(1) < nu_ref[0]

    @pl.when(used)
    def _():
        x = x_ref[...].astype(BF16)
        a = jnp.dot(x, w1_ref[0], preferred_element_type=F32)
        b = jnp.dot(x, w3_ref[0], preferred_element_type=F32)
        o_ref[...] = (_silu(a) * b).astype(o_ref.dtype)

    @pl.when(jnp.logical_not(used))
    def _():
        o_ref[...] = jnp.zeros(o_ref.shape, o_ref.dtype)


def _used(i, nu_ref):
    return jnp.minimum(i, nu_ref[0] - 1)


def moe_up(xs, w1, w3, tile_expert, n_used, idx):
    tm, tn = MOE_TM, UP_TN
    wspec = pl.BlockSpec((None, 1, D, tn), lambda j, i, te, nu: (idx, te[_used(i, nu)], 0, j))
    return pl.pallas_call(
        _moe_up_kernel,
        grid_spec=pltpu.PrefetchScalarGridSpec(
            num_scalar_prefetch=2,
            grid=(DFF // tn, MOE_TILES),
            in_specs=[pl.BlockSpec((tm, D), lambda j, i, te, nu: (_used(i, nu), 0)), wspec, wspec],
            out_specs=pl.BlockSpec((tm, tn), lambda j, i, te, nu: (i, j))),
        out_shape=jax.ShapeDtypeStruct((MOE_ROWS, DFF), BF16),
        compiler_params=_cp("arbitrary", "arbitrary"),
        name="moe_up",
    )(tile_expert, n_used, xs, w1, w3)


def _moe_down_kernel(te_ref, nu_ref, h_ref, w_ref, o_ref):
    used = pl.program_id(1) < nu_ref[0]

    @pl.when(used)
    def _():
        o_ref[...] = jnp.dot(h_ref[...], w_ref[0], preferred_element_type=F32)

    @pl.when(jnp.logical_not(used))
    def _():
        o_ref[...] = jnp.zeros(o_ref.shape, o_ref.dtype)


def moe_down(hs, w2, tile_expert, n_used, idx):
    tm, tn = MOE_TM, DOWN_TN
    return pl.pallas_call(
        _moe_down_kernel,
        grid_spec=pltpu.PrefetchScalarGridSpec(
            num_scalar_prefetch=2,
            grid=(D // tn, MOE_TILES),
            in_specs=[pl.BlockSpec((tm, DFF), lambda j, i, te, nu: (_used(i, nu), 0)),
                      pl.BlockSpec((None, 1, DFF, tn), lambda j, i, te, nu: (idx, te[_used(i, nu)], 0, j))],
            out_specs=pl.BlockSpec((tm, tn), lambda j, i, te, nu: (i, j))),
        out_shape=jax.ShapeDtypeStruct((MOE_ROWS, D), F32),
        compiler_params=_cp("arbitrary", "arbitrary"),
        name="moe_down",
    )(tile_expert, n_used, hs, w2)


COMB_TM = BLK


def _moe_combine_kernel(s1_ref, s2_ref, x_ref, route_ref, g_ref, ys_ref, yp_ref, ysm_ref, ya_ref, yb_ref, sem):
    i = pl.program_id(0)
    tm = COMB_TM

    def issue(r, carry):
        t = i * tm + r
        _row_copy(ys_ref, s1_ref[t], ya_ref, r, sem).start()
        _row_copy(ys_ref, s2_ref[t], yb_ref, r, sem).start()
        return carry

    lax.fori_loop(0, tm, issue, 0, unroll=8)

    def drain(r, carry):
        _row_copy(ys_ref, 0, ya_ref, 0, sem).wait()
        _row_copy(ys_ref, 0, yb_ref, 0, sem).wait()
        return carry

    lax.fori_loop(0, tm, drain, 0, unroll=8)
    w1 = route_ref[:, 2:3]
    w2 = route_ref[:, 3:4]
    x2 = x_ref[...] + (w1 * ya_ref[...] + w2 * yb_ref[...])
    y = _rms(x2, g_ref[...])
    is_prompt = i < B * NB

    @pl.when(is_prompt & (i % NB > 0))
    def _():
        yp_ref[...] = y

    @pl.when(jnp.logical_not(is_prompt))
    def _():
        ysm_ref[...] = y


def moe_combine(x1, route, g, ys, slot1, slot2):
    tm = COMB_TM
    return pl.pallas_call(
        _moe_combine_kernel,
        grid_spec=pltpu.PrefetchScalarGridSpec(
            num_scalar_prefetch=2,
            grid=(M // tm,),
            in_specs=[pl.BlockSpec((tm, D), lambda i, *_: (i, 0)), pl.BlockSpec((tm, ROUTE_W), lambda i, *_: (i, 0)),
                      pl.BlockSpec((1, D), lambda i, *_: (0, 0)), pl.BlockSpec(memory_space=pl.ANY)],
            out_specs=[pl.BlockSpec((tm, D), lambda i, *_: (_prompt_block(i), 0)),
                       pl.BlockSpec((tm, D), lambda i, *_: (jnp.maximum(i - B * NB, 0), 0))],
            scratch_shapes=[pltpu.VMEM((tm, D), F32), pltpu.VMEM((tm, D), F32), pltpu.SemaphoreType.DMA(())]),
        out_shape=[jax.ShapeDtypeStruct((B * SEQ, D), F32), jax.ShapeDtypeStruct((MS, D), F32)],
        compiler_params=_cp("arbitrary"),
        name="moe_combine",
    )(slot1, slot2, x1, route, g.reshape(1, D), ys)


def _moe_plan(route):
    e1 = route[:, 0].astype(jnp.int32)
    e2 = route[:, 1].astype(jnp.int32)
    ids = jnp.arange(NE, dtype=jnp.int32)
    sel = (e1[:, None] == ids).astype(jnp.int32) + (e2[:, None] == ids).astype(jnp.int32)
    incl = jnp.cumsum(sel, axis=0)
    rank = incl - sel
    cnt = incl[-1]
    ntile = (cnt + MOE_TM - 1) // MOE_TM
    tile_end = jnp.cumsum(ntile)
    tile_off = tile_end - ntile
    row_off = tile_off * MOE_TM
    slot1 = jnp.take(row_off, e1) + jnp.take_along_axis(rank, e1[:, None], axis=1)[:, 0]
    slot2 = jnp.take(row_off, e2) + jnp.take_along_axis(rank, e2[:, None], axis=1)[:, 0]
    tiles = jnp.arange(MOE_TILES, dtype=jnp.int32)
    tile_expert = jnp.minimum(jnp.sum((tiles[:, None] >= tile_end[None, :]).astype(jnp.int32), axis=1), NE - 1)
    n_used = tile_end[-1:].astype(jnp.int32)
    tail = jnp.minimum(tile_end[-1] + ids, MOE_TILES - 1)
    zero_tiles = jnp.concatenate([jnp.maximum(tile_end - 1, 0), tail]).astype(jnp.int32)
    return slot1.astype(jnp.int32), slot2.astype(jnp.int32), tile_expert.astype(jnp.int32), n_used, zero_tiles


def _blockdiag(w):
    per = GATE_CH // (LW // 16)
    w4 = w.reshape(LW // GATE_CH, per, 64, 64)
    eye = jnp.eye(per, dtype=w.dtype)
    return jnp.einsum("cnde,nm->cndme", w4, eye).reshape(LW // GATE_CH, GATE_CH, GATE_CH).astype(BF16)


def kernel(x_prompt, x_sample, cache_attn_k, cache_attn_v, state_conv, state_lru_conv, state_lru_h, meta_tokens,
           norm_mix, norm_ffn, norm_final, w_in, attn_sinks, w_attn_out, conv_dw_w, conv_dw_b, conv_ln_g, conv_ln_b,
           w_conv_out, lru_conv_w, lru_conv_b, lru_w_r, lru_b_r, lru_w_i, lru_b_i, lru_lambda, w_lru_out, w_mix_out,
           ffn_w1, ffn_w3, ffn_w2, moe_router_w, moe_router_b, moe_w1, moe_w3, moe_w2):
    ck = cache_attn_k.reshape(DEPTH * DB, WIN, KVW)
    cv = cache_attn_v.reshape(DEPTH * DB, WIN, KVW)
    sc = state_conv.reshape(DEPTH * DB, CK - 1, CW)
    slc = state_lru_conv.reshape(DEPTH * DB, LCK - 1, LW)
    slh = state_lru_h.reshape(DEPTH * DB, LW)

    s_k = jnp.zeros((DEPTH * DB, WIN, KVW), F32)
    s_v = jnp.zeros((DEPTH * DB, WIN, KVW), F32)
    s_conv = jnp.zeros((DEPTH * DB, CK - 1, CW), F32)
    s_lconv = jnp.zeros((DEPTH * DB, LCK - 1, LW), F32)
    s_h = jnp.zeros((DEPTH * DB, LW), F32)
    p_k, p_v, p_conv, p_lconv, p_h = [], [], [], [], []

    assert DEPTH == 2, "layer 0 dense SwiGLU, layer 1 routed experts followed by the final norm"
    w_in_b = w_in.astype(BF16)
    wa, wc, wl, wm = (w.astype(BF16) for w in (w_attn_out, w_conv_out, w_lru_out, w_mix_out))
    ffn_b = tuple(w.astype(BF16) for w in (ffn_w1, ffn_w3, ffn_w2))
    moe_side = {0: moe_w1.reshape(-1, DFF), 1: moe_w3.reshape(-1, DFF)}
    moe_b = {}

    x, xn = embed_norm(x_prompt, x_sample, meta_tokens, norm_mix[0])
    y_prompt = y_sample = None
    for l in range(DEPTH):
        z, side_b = inproj(xn, w_in_b, l, moe_side[l])
        moe_b[l] = side_b.reshape(moe_w1.shape)
        sinks = attn_sinks[l]

        o_p = attn_prompt(z, sinks)
        o_s, s_k, s_v = attn_sample(z, sinks, ck, cv, s_k, s_v, l)
        o = (o_p, o_s)
        kv_last = jnp.stack([lax.slice(z, ((b + 1) * PB - WIN, Z_K), ((b + 1) * PB, Z_K + 2 * KVW))
                             for b in range(B)]).astype(F32)
        p_k.append(kv_last[:, :, :KVW].reshape(B, WIN, NKV, HD))
        p_v.append(kv_last[:, :, KVW:].reshape(B, WIN, NKV, HD))

        c_p, pc = conv_prompt(z, conv_dw_w[l], conv_dw_b[l], conv_ln_g[l], conv_ln_b[l])
        c_s, s_conv = conv_sample(z, sc, conv_dw_w[l], conv_dw_b[l], conv_ln_g[l], conv_ln_b[l], s_conv, l)
        c = (c_p, c_s)
        p_conv.append(pc)

        wr = _blockdiag(lru_w_r[l])
        wi = _blockdiag(lru_w_i[l])
        lru_args = (lru_conv_w[l], lru_conv_b[l], wr, lru_b_r[l], wi, lru_b_i[l], lru_lambda[l])
        rr_p, plc, ph = lru_prompt(z, *lru_args)
        rr_s, s_lconv, s_h = lru_sample(z, slc, slh, *lru_args, s_lconv, s_h, l)
        rr = (rr_p, rr_s)
        p_lconv.append(plc)
        p_h.append(ph.reshape(B, LW))

        j = l // 2
        if l % 2 == 0:
            x1, hn = mix(o, c, rr, z, x, wa, wc, wl, wm, norm_ffn[l], l)
            h, w2_b = ffn_up(hn, ffn_b[0], ffn_b[1], j, moe_w2.reshape(-1, D))
            moe_b[2] = w2_b.reshape(moe_w2.shape)
            x, xn = ffn_down(h, ffn_b[2], x1, norm_mix[l + 1], j)
        else:
            rw = jnp.pad(moe_router_w[j], ((0, 0), (0, ROUTE_W - NE)))
            rw_hi = rw.astype(BF16)
            rw_lo = (rw - rw_hi.astype(F32)).astype(BF16)
            rb = jnp.pad(moe_router_b[j].reshape(1, NE), ((0, 0), (0, ROUTE_W - NE)))
            x1, route = mix(o, c, rr, z, x, wa, wc, wl, wm, norm_ffn[l], l,
                            router=(jnp.concatenate([rw_hi, rw_lo], axis=1), rb))
            slot1, slot2, tile_expert, n_used, zero_tiles = _moe_plan(route)
            xs = moe_scatter(x1, norm_ffn[l], slot1, slot2, zero_tiles)
            hs = moe_up(xs, moe_b[0], moe_b[1], tile_expert, n_used, j)
            ys = moe_down(hs, moe_b[2], tile_expert, n_used, j)
            y_prompt, y_sample = moe_combine(x1, route, norm_final, ys, slot1, slot2)

    y_prompt = y_prompt.reshape(B, SEQ, D)
    y_sample = y_sample.reshape(DB, T, D)
    st4 = lambda a: a.reshape(DEPTH, DB, WIN, NKV, HD)
    return (y_prompt, y_sample,
            jnp.stack(p_k), jnp.stack(p_v), jnp.stack(p_conv), jnp.stack(p_lconv), jnp.stack(p_h),
            st4(s_k), st4(s_v), s_conv.reshape(DEPTH, DB, CK - 1, CW), s_lconv.reshape(DEPTH, DB, LCK - 1, LW),
            s_h.reshape(DEPTH, DB, LW))
```

```python
import functools
import math

import jax
import jax.numpy as jnp
from jax import lax
from jax.experimental import pallas as pl
from jax.experimental.pallas import tpu as pltpu

F32 = jnp.float32
BF16 = jnp.bfloat16

D = 2048
B = 2
SEQ = 4096
DEPTH = 2
DB = 128
T = 8
PAST = 8192
N_META = 16
NH = 16
NKV = 4
G = NH // NKV
HD = 64
AW = NH * HD
KVW = NKV * HD
WIN = 128
BLK = 128
CW = D // 2
CK = 31
LW = D // 2
LCK = 4
LRU_C = 8.0
DFF = 5632
NE = 8
EPS = 1e-6
NEG = -1e30
IN_W = AW + 2 * KVW + 2 * CW + 2 * LW + 3 * D

PAD = (-N_META) % BLK
PB = PAD + N_META + SEQ
NB = PB // BLK
MP = B * PB
MS = DB * T
M = MP + MS

ZC = 512
NZC = IN_W // ZC
Z_Q, Z_A, Z_GATE, Z_LX, Z_LG, Z_GA, Z_K, Z_V = 0, 1024, 2048, 3072, 4096, 5120, 11264, 11520

SLOPES = tuple(2.0 ** (-8.0 * (h + 1.0) / NH) for h in range(NH))

TM_BIG = 1184
TM_MID = 592
TM_TOK = 256
SB = 8
MOE_TM = 256
MOE_TILES = (2 * M) // MOE_TM + NE
MOE_ROWS = MOE_TILES * MOE_TM
UP_TN = 1408
DOWN_TN = 1024

VMEM_LIMIT = 56 * 1024 * 1024


def _cp(*sem):
    return pltpu.CompilerParams(dimension_semantics=sem, vmem_limit_bytes=VMEM_LIMIT)


def _rms(x, g):
    return x * lax.rsqrt(jnp.mean(x * x, axis=-1, keepdims=True) + EPS) * g


def _sigmoid(x):
    return 1.0 / (1.0 + jnp.exp(-x))


def _silu(x):
    return x * _sigmoid(x)


def _prompt_block(i):
    ic = jnp.minimum(i, B * NB - 1)
    return (ic // NB) * (NB - 1) + jnp.maximum(ic % NB - 1, 0)


def _embed_norm_kernel(xp_ref, xs_ref, meta_ref, g_ref, x_ref, xn_ref):
    i = pl.program_id(0)
    head = jnp.concatenate([jnp.zeros((PAD, D), F32), meta_ref[...]], axis=0)
    x = jnp.where(i % NB == 0, head, xp_ref[...])
    x = jnp.where(i < B * NB, x, xs_ref[...])
    x_ref[...] = x
    xn_ref[...] = _rms(x, g_ref[...]).astype(xn_ref.dtype)


def embed_norm(x_prompt, x_sample, meta, g):
    tm = BLK
    return pl.pallas_call(
        _embed_norm_kernel,
        grid=(M // tm,),
        in_specs=[pl.BlockSpec((tm, D), lambda i: (_prompt_block(i), 0)),
                  pl.BlockSpec((tm, D), lambda i: (jnp.maximum(i - B * NB, 0), 0)),
                  pl.BlockSpec((N_META, D), lambda i: (0, 0)), pl.BlockSpec((1, D), lambda i: (0, 0))],
        out_specs=[pl.BlockSpec((tm, D), lambda i: (i, 0)), pl.BlockSpec((tm, D), lambda i: (i, 0))],
        out_shape=[jax.ShapeDtypeStruct((M, D), F32), jax.ShapeDtypeStruct((M, D), BF16)],
        compiler_params=_cp("parallel"),
        name="embed_norm",
    )(x_prompt.reshape(B * SEQ, D), x_sample.reshape(MS, D), meta, g.reshape(1, D))


def _cast_side(src_ref, dst_ref, step, nsteps):
    @pl.when(step < nsteps)
    def _():
        dst_ref[...] = src_ref[...].astype(dst_ref.dtype)


def _side_specs(shape2d, rows, step_of):
    nblk = shape2d[0] // rows
    assert nblk * rows == shape2d[0]
    spec = pl.BlockSpec((rows, shape2d[1]), lambda *g: (jnp.minimum(step_of(*g), nblk - 1), 0))
    return nblk, spec


def _inproj_kernel(x_ref, w_ref, side_ref, o_ref, side_out_ref, *, side_steps):
    o_ref[...] = jnp.dot(x_ref[...], w_ref[...], preferred_element_type=F32).astype(o_ref.dtype)
    _cast_side(side_ref, side_out_ref, pl.program_id(0) * NZC + pl.program_id(1), side_steps)


def _z_src_block(j):
    return jnp.where(j < 2, j, jnp.where(j < NZC - 1, j + 1, 2))


SIDE_ROWS_INPROJ = 128


def inproj(xn, w, layer, side):
    tm = TM_BIG
    nblk, sspec = _side_specs(side.shape, SIDE_ROWS_INPROJ, lambda i, j: i * NZC + j)
    assert nblk <= (M // tm) * NZC
    return pl.pallas_call(
        functools.partial(_inproj_kernel, side_steps=nblk),
        grid=(M // tm, NZC),
        in_specs=[pl.BlockSpec((tm, D), lambda i, j: (i, 0)),
                  pl.BlockSpec((None, D, ZC), lambda i, j: (layer, 0, _z_src_block(j))), sspec],
        out_specs=[pl.BlockSpec((tm, ZC), lambda i, j: (i, j)), sspec],
        out_shape=[jax.ShapeDtypeStruct((M, IN_W), BF16), jax.ShapeDtypeStruct(side.shape, BF16)],
        compiler_params=_cp("arbitrary", "arbitrary"),
        name="inproj",
    )(xn, w, side)


def _softmax_sink(s, sink):
    m = jnp.maximum(jnp.max(s, axis=-1, keepdims=True), sink)
    e = jnp.exp(s - m)
    l = jnp.sum(e, axis=-1, keepdims=True) + jnp.exp(sink - m)
    return e / l


def _attn_prompt_start(n, bias_ref):
    @pl.when(n == 0)
    def _():
        r = lax.broadcasted_iota(jnp.int32, (BLK, 2 * BLK), 0)
        c = lax.broadcasted_iota(jnp.int32, (BLK, 2 * BLK), 1)
        dist = BLK + r - c
        in_window = (dist >= 0) & (dist <= WIN)
        distf = dist.astype(F32)
        for hq in range(NH):
            bias_ref[hq] = jnp.where(in_window, -SLOPES[hq] * distf, NEG)


def _attn_prompt_main(n, sink_ref, q_ref, kp_ref, kc_ref, vp_ref, vc_ref, o_ref, bias_ref):
    c = lax.broadcasted_iota(jnp.int32, (BLK, 2 * BLK), 1)
    exists = (n - 1) * BLK - PAD + c >= 0
    q = q_ref[...] * (HD ** -0.5)
    k = jnp.concatenate([kp_ref[...], kc_ref[...]], axis=0)
    v = jnp.concatenate([vp_ref[...], vc_ref[...]], axis=0)
    for h in range(NKV):
        kh = k[:, h * HD:(h + 1) * HD]
        vh = v[:, h * HD:(h + 1) * HD]
        qh = jnp.concatenate([q[:, (h * G + g) * HD:(h * G + g + 1) * HD] for g in range(G)], axis=0)
        s = lax.dot_general(qh, kh, (((1,), (1,)), ((), ())), preferred_element_type=F32)
        ps = []
        for g in range(G):
            hq = h * G + g
            sg = jnp.where(exists, s[g * BLK:(g + 1) * BLK] + bias_ref[hq], NEG)
            ps.append(_softmax_sink(sg, sink_ref[hq]).astype(BF16))
        oh = jnp.dot(jnp.concatenate(ps, axis=0), vh, preferred_element_type=F32)
        for g in range(G):
            hq = h * G + g
            o_ref[:, hq * HD:(hq + 1) * HD] = oh[g * BLK:(g + 1) * BLK].astype(o_ref.dtype)


def _attn_sample_body(sink_ref, q_ref, kn_ref, vn_ref, ck_ref, cv_ref, o_ref, nk_ref, nv_ref):
    q = q_ref[...].astype(F32).reshape(SB, T, AW)
    kn = kn_ref[...].astype(F32).reshape(SB, T, KVW)
    vn = vn_ref[...].astype(F32).reshape(SB, T, KVW)
    ck = ck_ref[...]
    cv = cv_ref[...]
    wb = WIN
    nk_ref[:, 0:wb - T, :] = ck[:, T:wb, :]
    nk_ref[:, wb - T:wb, :] = kn
    nv_ref[:, 0:wb - T, :] = cv[:, T:wb, :]
    nv_ref[:, wb - T:wb, :] = vn
    zpad = jnp.zeros((SB, wb - T, KVW), F32)
    k = jnp.concatenate([ck, kn, zpad], axis=1).astype(BF16)
    v = jnp.concatenate([cv, vn, zpad], axis=1).astype(BF16)
    nkeys = 2 * wb
    r = lax.broadcasted_iota(jnp.int32, (G * T, nkeys), 0)
    c = lax.broadcasted_iota(jnp.int32, (G * T, nkeys), 1)
    dist = wb + (r % T) - c
    allowed = (dist >= 0) & (dist <= WIN)
    distf = dist.astype(F32)
    gidx = r // T
    for h in range(NKV):
        slope = jnp.zeros((G * T, nkeys), F32)
        sink = jnp.zeros((G * T, 1), F32)
        for g in range(G):
            slope = jnp.where(gidx == g, SLOPES[h * G + g], slope)
            sink = jnp.where(gidx[:, 0:1] == g, sink_ref[h * G + g], sink)
        kh = k[:, :, h * HD:(h + 1) * HD]
        vh = v[:, :, h * HD:(h + 1) * HD]
        qh = jnp.concatenate([q[:, :, (h * G + g) * HD:(h * G + g + 1) * HD] for g in range(G)],
                             axis=1).astype(BF16)
        s = jnp.einsum("bqd,bkd->bqk", qh, kh, preferred_element_type=F32)
        s = s * (HD ** -0.5) - (slope * distf)[None]
        s = jnp.where(allowed[None], s, NEG)
        p = _softmax_sink(s, sink[None]).astype(BF16)
        oh = jnp.einsum("bqk,bkd->bqd", p, vh, preferred_element_type=F32)
        for g in range(G):
            hq = h * G + g
            piece = oh[:, g * T:(g + 1) * T, :].reshape(SB * T, HD)
            o_ref[:, hq * HD:(hq + 1) * HD] = piece.astype(o_ref.dtype)


def _ln_swish(y, g, b):
    mu = jnp.mean(y, axis=-1, keepdims=True)
    yc = y - mu
    var = jnp.mean(yc * yc, axis=-1, keepdims=True)
    yn = yc * lax.rsqrt(var + EPS) * g + b
    return _silu(yn)


CONV_HIST = 32
SUBLANES = 8


def _dwconv_by_phase(load, wrow, nrows, first, ntaps):
    y = None
    for s in range(SUBLANES):
        n = nrows if s == 0 else nrows + SUBLANES
        part = None
        for q in range((first + ntaps - 1) // SUBLANES + 1):
            o = SUBLANES * q + s
            if first <= o < first + ntaps:
                term = wrow(o - first) * load(SUBLANES * q, n)
                part = term if part is None else part + term
        if part is not None:
            part = part[s:s + nrows]
            y = part if y is None else y + part
    return y


def _conv_prompt_start(n, ext_ref):
    @pl.when(n == 0)
    def _():
        ext_ref[0:CONV_HIST, :] = jnp.zeros((CONV_HIST, CW), F32)

    @pl.when(n > 0)
    def _():
        ext_ref[0:CONV_HIST, :] = ext_ref[BLK:BLK + CONV_HIST, :]


def _conv_prompt_main(n, a_ref, g_ref, w_ref, b_ref, lg_ref, lb_ref, c_ref, ext_ref, y_ref):
    u = a_ref[...].astype(F32) * _sigmoid(g_ref[...].astype(F32))
    row = lax.broadcasted_iota(jnp.int32, (BLK, 1), 0)
    u = jnp.where((n > 0) | (row >= PAD), u, 0.0)
    ext_ref[CONV_HIST:CONV_HIST + BLK, :] = u
    off = CONV_HIST - (CK - 1)
    cc = 256
    for c0 in range(0, CW, cc):
        cs = slice(c0, c0 + cc)
        y_ref[:, cs] = b_ref[:, cs] + _dwconv_by_phase(lambda r0, nr: ext_ref[r0:r0 + nr, cs],
                                                      lambda j: w_ref[j:j + 1, cs], BLK, off, CK)
    c_ref[...] = _ln_swish(y_ref[...], lg_ref[...], lb_ref[...]).astype(c_ref.dtype)


def _conv_prompt_finish(n, st_ref, ext_ref):
    @pl.when(n == NB - 1)
    def _():
        st_ref[0] = ext_ref[CONV_HIST + BLK - (CK - 1):CONV_HIST + BLK, :]


def _conv_sample_body(a_ref, g_ref, st_ref, w_ref, b_ref, lg_ref, lb_ref, c_ref, nst_ref, ext_ref, y_ref):
    u = a_ref[...].astype(F32) * _sigmoid(g_ref[...].astype(F32))
    hist = CK - 1
    base = CONV_HIST - hist
    for s in range(SB):
        ext_ref[s, 0:base, :] = jnp.zeros((base, CW), F32)
        ext_ref[s, base:CONV_HIST, :] = st_ref[s]
        ext_ref[s, CONV_HIST:CONV_HIST + T, :] = u[s * T:(s + 1) * T]
    for s in range(SB):
        y_ref[s * T:(s + 1) * T, :] = b_ref[...] + _dwconv_by_phase(
            lambda r0, nr: ext_ref[s, r0:r0 + nr, :], lambda j: w_ref[j:j + 1, :], T, base, CK)
        nst_ref[s] = ext_ref[s, base + T:base + T + hist, :]
    c_ref[...] = _ln_swish(y_ref[...], lg_ref[...], lb_ref[...]).astype(c_ref.dtype)


LRU_HIST = 8
GATE_CH = 256


def _gelu_tanh(x):
    return x * (0.5 * (1.0 + jnp.tanh(math.sqrt(2.0 / math.pi) * (x + 0.044715 * (x * x * x)))))


def _softplus(x):
    return jnp.maximum(x, 0.0) + jnp.log1p(jnp.exp(-jnp.abs(x)))


def _expm1(x):
    return jnp.tanh(0.5 * x) * (jnp.exp(x) + 1.0)


def _lru_gates(xc, wr_ref, br_ref, wi_ref, bi_ref, lam_ref, valid, a_ref, bx_ref):
    sp = _softplus(-lam_ref[...])
    for k in range(LW // GATE_CH):
        sl = slice(k * GATE_CH, (k + 1) * GATE_CH)
        xk = xc[:, sl]
        xkb = xk.astype(BF16)
        r = _sigmoid(jnp.dot(xkb, wr_ref[k], preferred_element_type=F32) + br_ref[:, sl])
        i = _sigmoid(jnp.dot(xkb, wi_ref[k], preferred_element_type=F32) + bi_ref[:, sl])
        log_a = (-LRU_C) * r * sp[:, sl]
        a = jnp.exp(log_a)
        bx = jnp.sqrt(-_expm1(2.0 * log_a)) * (i * xk)
        if valid is not None:
            bx = jnp.where(valid, bx, 0.0)
        a_ref[:, sl] = a
        bx_ref[:, sl] = bx


def _lru_prompt_start(n, ext_ref, carry_ref):
    @pl.when(n == 0)
    def _():
        ext_ref[0:LRU_HIST, :] = jnp.zeros((LRU_HIST, LW), F32)
        carry_ref[...] = jnp.zeros((1, LW), F32)

    @pl.when(n > 0)
    def _():
        ext_ref[0:LRU_HIST, :] = ext_ref[BLK:BLK + LRU_HIST, :]


def _lru_prompt_main(n, x_ref, g_ref, cw_ref, cb_ref, wr_ref, br_ref, wi_ref, bi_ref, lam_ref,
                     rr_ref, ext_ref, a_ref, bx_ref, h_ref, carry_ref):
    row = lax.broadcasted_iota(jnp.int32, (BLK, 1), 0)
    valid = (n > 0) | (row >= PAD)
    ext_ref[LRU_HIST:LRU_HIST + BLK, :] = jnp.where(valid, x_ref[...].astype(F32), 0.0)
    off = LRU_HIST - (LCK - 1)
    xc = jnp.zeros((BLK, LW), F32) + cb_ref[...]
    for j in range(LCK):
        xc = xc + cw_ref[j:j + 1, :] * ext_ref[off + j:off + j + BLK, :]
    _lru_gates(xc, wr_ref, br_ref, wi_ref, bi_ref, lam_ref, valid, a_ref, bx_ref)

    h = carry_ref[...]
    for t in range(BLK):
        h = a_ref[t:t + 1, :] * h + bx_ref[t:t + 1, :]
        h_ref[t:t + 1, :] = h
    carry_ref[...] = h
    rr_ref[...] = (h_ref[...] * _gelu_tanh(g_ref[...].astype(F32))).astype(rr_ref.dtype)


def _lru_prompt_finish(n, cst_ref, hst_ref, ext_ref, carry_ref):
    @pl.when(n == NB - 1)
    def _():
        cst_ref[0] = ext_ref[LRU_HIST + BLK - (LCK - 1):LRU_HIST + BLK, :]
        hst_ref[0] = carry_ref[...]


def _prompt_branches_kernel(sink_ref, q_ref, kp_ref, kc_ref, vp_ref, vc_ref, ca_ref, cg_ref, lx_ref, lg_ref,
                            dw_ref, db_ref, lng_ref, lnb_ref,
                            cw_ref, cb_ref, wr_ref, br_ref, wi_ref, bi_ref, lam_ref,
                            o_ref, c_ref, rr_ref, cst_ref, lcst_ref, hst_ref,
                            bias_ref, cext_ref, cy_ref, lext_ref, a_ref, bx_ref, h_ref, carry_ref):
    n = pl.program_id(1)
    _attn_prompt_start(n, bias_ref)
    _conv_prompt_start(n, cext_ref)
    _lru_prompt_start(n, lext_ref, carry_ref)
    _attn_prompt_main(n, sink_ref, q_ref, kp_ref, kc_ref, vp_ref, vc_ref, o_ref, bias_ref)
    _conv_prompt_main(n, ca_ref, cg_ref, dw_ref, db_ref, lng_ref, lnb_ref, c_ref, cext_ref, cy_ref)
    _lru_prompt_main(n, lx_ref, lg_ref, cw_ref, cb_ref, wr_ref, br_ref, wi_ref, bi_ref, lam_ref,
                     rr_ref, lext_ref, a_ref, bx_ref, h_ref, carry_ref)
    _conv_prompt_finish(n, cst_ref, cext_ref)
    _lru_prompt_finish(n, lcst_ref, hst_ref, lext_ref, carry_ref)


def prompt_branches(z, sinks, conv_p, lru_p):
    kb, vb = Z_K // KVW, Z_V // KVW
    cur = lambda col: (lambda b, n: (b * NB + n, col))
    prev = lambda col: (lambda b, n: (b * NB + jnp.maximum(n - 1, 0), col))
    full = lambda shape: pl.BlockSpec(shape, lambda b, n: (0,) * len(shape))
    wide = lambda col: pl.BlockSpec((BLK, 1024), cur(col))
    vec = full((1, 1024))
    wspec = full((LW // GATE_CH, GATE_CH, GATE_CH))
    dw_w, dw_b, ln_g, ln_b = conv_p
    cw, cb, wr, br, wi, bi, lam = lru_p
    row = lambda v: v.reshape(1, -1)
    blk = pltpu.VMEM((BLK, 1024), F32)
    return pl.pallas_call(
        _prompt_branches_kernel,
        grid=(B, NB),
        in_specs=[pl.BlockSpec(memory_space=pltpu.SMEM),
                  wide(0),
                  pl.BlockSpec((BLK, KVW), prev(kb)), pl.BlockSpec((BLK, KVW), cur(kb)),
                  pl.BlockSpec((BLK, KVW), prev(vb)), pl.BlockSpec((BLK, KVW), cur(vb)),
                  wide(Z_A // 1024), wide(Z_GATE // 1024), wide(Z_LX // 1024), wide(Z_LG // 1024),
                  full((CK, CW)), vec, vec, vec,
                  full((LCK, LW)), vec, wspec, vec, wspec, vec, vec],
        out_specs=[wide(0), wide(0), wide(0),
                   pl.BlockSpec((1, CK - 1, CW), lambda b, n: (b, 0, 0)),
                   pl.BlockSpec((1, LCK - 1, LW), lambda b, n: (b, 0, 0)),
                   pl.BlockSpec((1, 1, LW), lambda b, n: (b, 0, 0))],
        out_shape=[jax.ShapeDtypeStruct((MP, AW), BF16), jax.ShapeDtypeStruct((MP, CW), BF16),
                   jax.ShapeDtypeStruct((MP, LW), BF16), jax.ShapeDtypeStruct((B, CK - 1, CW), F32),
                   jax.ShapeDtypeStruct((B, LCK - 1, LW), F32), jax.ShapeDtypeStruct((B, 1, LW), F32)],
        scratch_shapes=[pltpu.VMEM((NH, BLK, 2 * BLK), F32),
                        pltpu.VMEM((CONV_HIST + BLK, CW), F32), blk,
                        pltpu.VMEM((LRU_HIST + BLK, LW), F32), blk, blk, blk, pltpu.VMEM((1, LW), F32)],
        compiler_params=_cp("parallel", "arbitrary"),
        name="prompt_branches",
    )(sinks, z, z, z, z, z, z, z, z, z, dw_w, row(dw_b), row(ln_g), row(ln_b),
      cw, row(cb), wr, row(br), wi, row(bi), row(lam))


def _lru_sample_body(x_ref, g_ref, cst_ref, h0_ref, cw_ref, cb_ref, wr_ref, br_ref, wi_ref, bi_ref, lam_ref,
                     rr_ref, ncst_ref, nh_ref, ext_ref, xc_ref, a_ref, bx_ref, h_ref):
    hist = LCK - 1
    base = LRU_HIST - hist
    x = x_ref[...].astype(F32)
    for s in range(SB):
        ext_ref[s, base:LRU_HIST, :] = cst_ref[s]
        ext_ref[s, LRU_HIST:LRU_HIST + T, :] = x[s * T:(s + 1) * T]
    for s in range(SB):
        acc = jnp.zeros((T, LW), F32) + cb_ref[...]
        for j in range(LCK):
            acc = acc + cw_ref[j:j + 1, :] * ext_ref[s, base + j:base + j + T, :]
        xc_ref[s * T:(s + 1) * T, :] = acc
        ncst_ref[s] = ext_ref[s, base + T:base + T + hist, :]
    _lru_gates(xc_ref[...], wr_ref, br_ref, wi_ref, bi_ref, lam_ref, None, a_ref, bx_ref)
    for s in range(SB):
        h = h0_ref[s:s + 1, :]
        for t in range(T):
            rw = s * T + t
            h = a_ref[rw:rw + 1, :] * h + bx_ref[rw:rw + 1, :]
            h_ref[rw:rw + 1, :] = h
        nh_ref[s:s + 1, :] = h
    rr_ref[...] = (h_ref[...] * _gelu_tanh(g_ref[...].astype(F32))).astype(rr_ref.dtype)


def _sample_branches_kernel(sink_ref, q_ref, kn_ref, vn_ref, ck_ref, cv_ref, ca_ref, cg_ref, cst_ref,
                            dw_ref, db_ref, lng_ref, lnb_ref, lx_ref, lg_ref, lcst_ref, h0_ref,
                            cw_ref, cb_ref, wr_ref, br_ref, wi_ref, bi_ref, lam_ref,
                            nk_in, nv_in, nst_in, ncst_in, nh_in,
                            o_ref, nk_ref, nv_ref, c_ref, nst_ref, rr_ref, ncst_ref, nh_ref,
                            cext_ref, cy_ref, lext_ref, xc_ref, a_ref, bx_ref, h_ref):
    del nk_in, nv_in, nst_in, ncst_in, nh_in
    _attn_sample_body(sink_ref, q_ref, kn_ref, vn_ref, ck_ref, cv_ref, o_ref, nk_ref, nv_ref)
    _conv_sample_body(ca_ref, cg_ref, cst_ref, dw_ref, db_ref, lng_ref, lnb_ref, c_ref, nst_ref, cext_ref, cy_ref)
    _lru_sample_body(lx_ref, lg_ref, lcst_ref, h0_ref, cw_ref, cb_ref, wr_ref, br_ref, wi_ref, bi_ref, lam_ref,
                     rr_ref, ncst_ref, nh_ref, lext_ref, xc_ref, a_ref, bx_ref, h_ref)


def sample_branches(z, sinks, conv_p, lru_p, states_in, states_out, layer):
    rows = SB * T
    rb0 = MP // rows
    sb0 = layer * (DB // SB)
    kb, vb = Z_K // KVW, Z_V // KVW
    zrow = lambda w, col: pl.BlockSpec((rows, w), lambda i: (rb0 + i, col))
    full = lambda shape: pl.BlockSpec(shape, lambda i: (0,) * len(shape))
    st3 = lambda n, w: pl.BlockSpec((SB, n, w), lambda i: (sb0 + i, 0, 0))
    vec = full((1, 1024))
    wspec = full((LW // GATE_CH, GATE_CH, GATE_CH))
    kvspec, cspec, lcspec = st3(WIN, KVW), st3(CK - 1, CW), st3(LCK - 1, LW)
    hspec = pl.BlockSpec((SB, LW), lambda i: (sb0 + i, 0))
    anyspec = pl.BlockSpec(memory_space=pl.ANY)
    out_row = pl.BlockSpec((rows, 1024), lambda i: (i, 0))
    dw_w, dw_b, ln_g, ln_b = conv_p
    cw, cb, wr, br, wi, bi, lam = lru_p
    row = lambda v: v.reshape(1, -1)
    blk = pltpu.VMEM((rows, 1024), F32)
    shape_of = lambda a: jax.ShapeDtypeStruct(a.shape, a.dtype)
    n_in = 24
    outs = pl.pallas_call(
        _sample_branches_kernel,
        grid=(DB // SB,),
        in_specs=[pl.BlockSpec(memory_space=pltpu.SMEM),
                  zrow(AW, 0), zrow(KVW, kb), zrow(KVW, vb), kvspec, kvspec,
                  zrow(CW, Z_A // CW), zrow(CW, Z_GATE // CW), cspec, full((CK, CW)), vec, vec, vec,
                  zrow(LW, Z_LX // LW), zrow(LW, Z_LG // LW), lcspec, hspec,
                  full((LCK, LW)), vec, wspec, vec, wspec, vec, vec] + [anyspec] * 5,
        out_specs=[out_row, kvspec, kvspec, out_row, cspec, out_row, lcspec, hspec],
        out_shape=[jax.ShapeDtypeStruct((MS, AW), BF16), shape_of(states_out[0]), shape_of(states_out[1]),
                   jax.ShapeDtypeStruct((MS, CW), BF16), shape_of(states_out[2]),
                   jax.ShapeDtypeStruct((MS, LW), BF16), shape_of(states_out[3]), shape_of(states_out[4])],
        scratch_shapes=[pltpu.VMEM((SB, CONV_HIST + T, CW), F32), blk,
                        pltpu.VMEM((SB, LRU_HIST + T, LW), F32), blk, blk, blk, blk],
        input_output_aliases={n_in: 1, n_in + 1: 2, n_in + 2: 4, n_in + 3: 6, n_in + 4: 7},
        compiler_params=_cp("parallel"),
        name="sample_branches",
    )(sinks, z, z, z, states_in[0], states_in[1], z, z, states_in[2], dw_w, row(dw_b), row(ln_g), row(ln_b),
      z, z, states_in[3], states_in[4], cw, row(cb), wr, row(br), wi, row(bi), row(lam), *states_out)
    o, nk, nv, c, nst, rr, ncst, nh = outs
    return o, c, rr, (nk, nv, nst, ncst, nh)


ROUTE_W = 128


def _mix_body(branch_refs, gates, x_ref, wa_ref, wc_ref, wl_ref, wm_ref, g_ref, x1_ref):
    is_prompt = pl.program_id(0) < MP // TM_TOK
    o, c, r = (jnp.where(is_prompt, p_ref[...], s_ref[...]) for p_ref, s_ref in branch_refs)
    half = D // 2
    parts = []
    for hh in range(2):
        sl = slice(hh * half, (hh + 1) * half)
        m = _sigmoid(gates[0][hh][...].astype(F32)) * jnp.dot(o, wa_ref[:, sl], preferred_element_type=F32)
        m = m + _sigmoid(gates[1][hh][...].astype(F32)) * jnp.dot(c, wc_ref[:, sl], preferred_element_type=F32)
        m = m + _sigmoid(gates[2][hh][...].astype(F32)) * jnp.dot(r, wl_ref[:, sl], preferred_element_type=F32)
        parts.append(m.astype(BF16))
    merged = jnp.concatenate(parts, axis=1)
    x1 = x_ref[...] + jnp.dot(merged, wm_ref[...], preferred_element_type=F32)
    x1_ref[...] = x1
    return _rms(x1, g_ref[...])


def _mix_kernel(op_ref, os_ref, cp_ref, cs_ref, rp_ref, rs_ref, ga0, ga1, gb0, gb1, gc0, gc1, x_ref,
                wa_ref, wc_ref, wl_ref, wm_ref, g_ref, x1_ref, hn_ref):
    hn = _mix_body(((op_ref, os_ref), (cp_ref, cs_ref), (rp_ref, rs_ref)), ((ga0, ga1), (gb0, gb1), (gc0, gc1)),
                   x_ref, wa_ref, wc_ref, wl_ref, wm_ref, g_ref, x1_ref)
    hn_ref[...] = hn.astype(hn_ref.dtype)


def _mix_router_kernel(op_ref, os_ref, cp_ref, cs_ref, rp_ref, rs_ref, ga0, ga1, gb0, gb1, gc0, gc1, x_ref,
                       wa_ref, wc_ref, wl_ref, wm_ref, g_ref, rw_ref, rb_ref, x1_ref, route_ref):
    hn = _mix_body(((op_ref, os_ref), (cp_ref, cs_ref), (rp_ref, rs_ref)), ((ga0, ga1), (gb0, gb1), (gc0, gc1)),
                   x_ref, wa_ref, wc_ref, wl_ref, wm_ref, g_ref, x1_ref)
    h_hi = hn.astype(BF16)
    h_lo = (hn - h_hi.astype(F32)).astype(BF16)
    both = jnp.dot(h_hi, rw_ref[...], preferred_element_type=F32)
    logits = (both[:, :ROUTE_W] + both[:, ROUTE_W:]
              + jnp.dot(h_lo, rw_ref[:, :ROUTE_W], preferred_element_type=F32) + rb_ref[...])
    lane = lax.broadcasted_iota(jnp.int32, logits.shape, 1)
    ninf = -jnp.inf
    l1 = jnp.where(lane < NE, logits, ninf)
    m1 = jnp.max(l1, axis=-1, keepdims=True)
    i1 = jnp.min(jnp.where(l1 == m1, lane, ROUTE_W), axis=-1, keepdims=True)
    l2 = jnp.where(lane == i1, ninf, l1)
    m2 = jnp.max(l2, axis=-1, keepdims=True)
    i2 = jnp.min(jnp.where(l2 == m2, lane, ROUTE_W), axis=-1, keepdims=True)
    e2 = jnp.exp(m2 - m1)
    den = 1.0 + e2
    out = jnp.where(lane == 0, i1.astype(F32), 0.0)
    out = jnp.where(lane == 1, i2.astype(F32), out)
    out = jnp.where(lane == 2, 1.0 / den, out)
    out = jnp.where(lane == 3, e2 / den, out)
    route_ref[...] = out


def mix(o, c, rr, z, x, wa, wc, wl, wm, g, layer, router=None):
    tm = TM_TOK
    half = D // 2
    npt = MP // tm
    row = lambda col: (lambda i: (i, col))
    const = lambda shape: pl.BlockSpec(shape, lambda i: (0,) * len(shape), pipeline_mode=pl.Buffered(1))
    wspec = lambda k: pl.BlockSpec((None, k, D), lambda i: (layer, 0, 0), pipeline_mode=pl.Buffered(1))
    gate_specs = [pl.BlockSpec((tm, half), row(Z_GA // half + k)) for k in range(6)]
    pair = lambda w: [pl.BlockSpec((tm, w), lambda i: (jnp.minimum(i, npt - 1), 0)),
                      pl.BlockSpec((tm, w), lambda i: (jnp.maximum(i - npt, 0), 0))]
    in_specs = (pair(AW) + pair(CW) + pair(LW) + gate_specs
                + [pl.BlockSpec((tm, D), row(0)), wspec(AW), wspec(CW), wspec(LW), wspec(D), const((1, D))])
    args = [*o, *c, *rr, z, z, z, z, z, z, x, wa, wc, wl, wm, g.reshape(1, D)]
    if router is None:
        kern = _mix_kernel
        out_specs = [pl.BlockSpec((tm, D), row(0)), pl.BlockSpec((tm, D), row(0))]
        out_shape = [jax.ShapeDtypeStruct((M, D), F32), jax.ShapeDtypeStruct((M, D), BF16)]
    else:
        kern = _mix_router_kernel
        rw, rb = router
        in_specs += [const((D, 2 * ROUTE_W)), const((1, ROUTE_W))]
        args += [rw, rb]
        out_specs = [pl.BlockSpec((tm, D), row(0)), pl.BlockSpec((tm, ROUTE_W), row(0))]
        out_shape = [jax.ShapeDtypeStruct((M, D), F32), jax.ShapeDtypeStruct((M, ROUTE_W), F32)]
    return pl.pallas_call(
        kern,
        grid=(M // tm,),
        in_specs=in_specs,
        out_specs=out_specs,
        out_shape=out_shape,
        compiler_params=_cp("parallel"),
        name="mix",
    )(*args)


def _ffn_up_kernel(x_ref, w1_ref, w3_ref, side_ref, o_ref, side_out_ref, *, side_steps):
    x = x_ref[...]
    a = jnp.dot(x, w1_ref[...], preferred_element_type=F32)
    b = jnp.dot(x, w3_ref[...], preferred_element_type=F32)
    o_ref[...] = (_silu(a) * b).astype(o_ref.dtype)
    _cast_side(side_ref, side_out_ref, pl.program_id(0) * pl.num_programs(1) + pl.program_id(1), side_steps)


SIDE_ROWS_FFN = 512


def ffn_up(hn, w1, w3, idx, side):
    tm, tn = TM_BIG, 512
    nj = DFF // tn
    wspec = pl.BlockSpec((None, D, tn), lambda i, j: (idx, 0, j))
    nblk, sspec = _side_specs(side.shape, SIDE_ROWS_FFN, lambda i, j: i * nj + j)
    assert nblk <= (M // tm) * nj
    return pl.pallas_call(
        functools.partial(_ffn_up_kernel, side_steps=nblk),
        grid=(M // tm, nj),
        in_specs=[pl.BlockSpec((tm, D), lambda i, j: (i, 0)), wspec, wspec, sspec],
        out_specs=[pl.BlockSpec((tm, tn), lambda i, j: (i, j)), sspec],
        out_shape=[jax.ShapeDtypeStruct((M, DFF), BF16), jax.ShapeDtypeStruct(side.shape, BF16)],
        compiler_params=_cp("arbitrary", "arbitrary"),
        name="ffn_up",
    )(hn, w1, w3, side)


def _ffn_down_kernel(h_ref, w_ref, x_ref, g_ref, x2_ref, xn_ref):
    x2 = x_ref[...] + jnp.dot(h_ref[...], w_ref[...], preferred_element_type=F32)
    x2_ref[...] = x2
    xn_ref[...] = _rms(x2, g_ref[...]).astype(xn_ref.dtype)


def ffn_down(h, w2, x1, g, idx):
    tm = TM_TOK
    return pl.pallas_call(
        _ffn_down_kernel,
        grid=(M // tm,),
        in_specs=[pl.BlockSpec((tm, DFF), lambda i: (i, 0)),
                  pl.BlockSpec((None, DFF, D), lambda i: (idx, 0, 0), pipeline_mode=pl.Buffered(1)),
                  pl.BlockSpec((tm, D), lambda i: (i, 0)), pl.BlockSpec((1, D), lambda i: (0, 0))],
        out_specs=[pl.BlockSpec((tm, D), lambda i: (i, 0)), pl.BlockSpec((tm, D), lambda i: (i, 0))],
        out_shape=[jax.ShapeDtypeStruct((M, D), F32), jax.ShapeDtypeStruct((M, D), BF16)],
        compiler_params=_cp("parallel"),
        name="ffn_down",
    )(h, w2, x1, g.reshape(1, D))


def _row_copy(src, src_row, dst, dst_row, sem):
    return pltpu.make_async_copy(src.at[pl.ds(src_row, 1)], dst.at[pl.ds(dst_row, 1)], sem)


def _moe_scatter_kernel(s1_ref, s2_ref, zt_ref, x_ref, g_ref, xs_ref, buf_ref, zero_ref, sem):
    i = pl.program_id(0)
    tm = MOE_TM
    slot = i % 2

    def tile_copy(e):
        return pltpu.make_async_copy(zero_ref, xs_ref.at[pl.ds(pl.multiple_of(zt_ref[e] * tm, tm), tm)],
                                     sem.at[0])

    @pl.when(i == 0)
    def _():
        zero_ref[...] = jnp.zeros(zero_ref.shape, F32)
        for e in range(2 * NE):
            tile_copy(e).start()
            tile_copy(e).wait()

    buf_ref[slot] = _rms(x_ref[...], g_ref[...])

    def issue(r, carry):
        t = i * tm + r
        _row_copy(buf_ref.at[slot], r, xs_ref, s1_ref[t], sem.at[slot]).start()
        _row_copy(buf_ref.at[slot], r, xs_ref, s2_ref[t], sem.at[slot]).start()
        return carry

    lax.fori_loop(0, tm, issue, 0, unroll=8)

    def drain_slot(sl):
        def drain(r, carry):
            _row_copy(buf_ref.at[sl], 0, xs_ref, 0, sem.at[sl]).wait()
            _row_copy(buf_ref.at[sl], 0, xs_ref, 0, sem.at[sl]).wait()
            return carry

        lax.fori_loop(0, tm, drain, 0, unroll=8)

    @pl.when(i > 0)
    def _():
        drain_slot(1 - slot)

    @pl.when(i == pl.num_programs(0) - 1)
    def _():
        drain_slot(slot)


def moe_scatter(x1, g, slot1, slot2, zero_tiles):
    tm = MOE_TM
    return pl.pallas_call(
        _moe_scatter_kernel,
        grid_spec=pltpu.PrefetchScalarGridSpec(
            num_scalar_prefetch=3,
            grid=(M // tm,),
            in_specs=[pl.BlockSpec((tm, D), lambda i, *_: (i, 0)), pl.BlockSpec((1, D), lambda i, *_: (0, 0))],
            out_specs=pl.BlockSpec(memory_space=pl.ANY),
            scratch_shapes=[pltpu.VMEM((2, tm, D), F32), pltpu.VMEM((tm, D), F32),
                            pltpu.SemaphoreType.DMA((2,))]),
        out_shape=jax.ShapeDtypeStruct((MOE_ROWS, D), F32),
        compiler_params=_cp("arbitrary"),
        name="moe_scatter",
    )(slot1, slot2, zero_tiles, x1, g.reshape(1, D))


def _moe_up_kernel(te_ref, nu_ref, x_ref, w1_ref, w3_ref, o_ref):
    used = pl.program_id(1) < nu_ref[0]

    @pl.when(used)
    def _():
        x = x_ref[...].astype(BF16)
        a = jnp.dot(x, w1_ref[0], preferred_element_type=F32)
        b = jnp.dot(x, w3_ref[0], preferred_element_type=F32)
        o_ref[...] = (_silu(a) * b).astype(o_ref.dtype)

    @pl.when(jnp.logical_not(used))
    def _():
        o_ref[...] = jnp.zeros(o_ref.shape, o_ref.dtype)


def _used(i, nu_ref):
    return jnp.minimum(i, nu_ref[0] - 1)


def moe_up(xs, w1, w3, tile_expert, n_used, idx):
    tm, tn = MOE_TM, UP_TN
    wspec = pl.BlockSpec((None, 1, D, tn), lambda j, i, te, nu: (idx, te[_used(i, nu)], 0, j))
    return pl.pallas_call(
        _moe_up_kernel,
        grid_spec=pltpu.PrefetchScalarGridSpec(
            num_scalar_prefetch=2,
            grid=(DFF // tn, MOE_TILES),
            in_specs=[pl.BlockSpec((tm, D), lambda j, i, te, nu: (_used(i, nu), 0)), wspec, wspec],
            out_specs=pl.BlockSpec((tm, tn), lambda j, i, te, nu: (i, j))),
        out_shape=jax.ShapeDtypeStruct((MOE_ROWS, DFF), BF16),
        compiler_params=_cp("arbitrary", "arbitrary"),
        name="moe_up",
    )(tile_expert, n_used, xs, w1, w3)


def _moe_down_kernel(te_ref, nu_ref, h_ref, w_ref, o_ref):
    used = pl.program_id(1) < nu_ref[0]

    @pl.when(used)
    def _():
        o_ref[...] = jnp.dot(h_ref[...], w_ref[0], preferred_element_type=F32)

    @pl.when(jnp.logical_not(used))
    def _():
        o_ref[...] = jnp.zeros(o_ref.shape, o_ref.dtype)


def moe_down(hs, w2, tile_expert, n_used, idx):
    tm, tn = MOE_TM, DOWN_TN
    return pl.pallas_call(
        _moe_down_kernel,
        grid_spec=pltpu.PrefetchScalarGridSpec(
            num_scalar_prefetch=2,
            grid=(D // tn, MOE_TILES),
            in_specs=[pl.BlockSpec((tm, DFF), lambda j, i, te, nu: (_used(i, nu), 0)),
                      pl.BlockSpec((None, 1, DFF, tn), lambda j, i, te, nu: (idx, te[_used(i, nu)], 0, j))],
            out_specs=pl.BlockSpec((tm, tn), lambda j, i, te, nu: (i, j))),
        out_shape=jax.ShapeDtypeStruct((MOE_ROWS, D), F32),
        compiler_params=_cp("arbitrary", "arbitrary"),
        name="moe_down",
    )(tile_expert, n_used, hs, w2)


COMB_TM = BLK


def _moe_combine_kernel(s1_ref, s2_ref, x_ref, route_ref, g_ref, ys_ref, yp_ref, ysm_ref, ya_ref, yb_ref, sem):
    i = pl.program_id(0)
    tm = COMB_TM
    slot = i % 2

    def fetch(tile, sl):
        def issue(r, carry):
            t = tile * tm + r
            _row_copy(ys_ref, s1_ref[t], ya_ref.at[sl], r, sem.at[sl]).start()
            _row_copy(ys_ref, s2_ref[t], yb_ref.at[sl], r, sem.at[sl]).start()
            return carry

        lax.fori_loop(0, tm, issue, 0, unroll=8)

    @pl.when(i == 0)
    def _():
        fetch(0, 0)

    @pl.when(i + 1 < pl.num_programs(0))
    def _():
        fetch(i + 1, 1 - slot)

    def drain(r, carry):
        _row_copy(ys_ref, 0, ya_ref.at[slot], 0, sem.at[slot]).wait()
        _row_copy(ys_ref, 0, yb_ref.at[slot], 0, sem.at[slot]).wait()
        return carry

    lax.fori_loop(0, tm, drain, 0, unroll=8)
    w1 = route_ref[:, 2:3]
    w2 = route_ref[:, 3:4]
    x2 = x_ref[...] + (w1 * ya_ref[slot] + w2 * yb_ref[slot])
    y = _rms(x2, g_ref[...])
    is_prompt = i < B * NB

    @pl.when(is_prompt & (i % NB > 0))
    def _():
        yp_ref[...] = y

    @pl.when(jnp.logical_not(is_prompt))
    def _():
        ysm_ref[...] = y


def moe_combine(x1, route, g, ys, slot1, slot2):
    tm = COMB_TM
    return pl.pallas_call(
        _moe_combine_kernel,
        grid_spec=pltpu.PrefetchScalarGridSpec(
            num_scalar_prefetch=2,
            grid=(M // tm,),
            in_specs=[pl.BlockSpec((tm, D), lambda i, *_: (i, 0)), pl.BlockSpec((tm, ROUTE_W), lambda i, *_: (i, 0)),
                      pl.BlockSpec((1, D), lambda i, *_: (0, 0)), pl.BlockSpec(memory_space=pl.ANY)],
            out_specs=[pl.BlockSpec((tm, D), lambda i, *_: (_prompt_block(i), 0)),
                       pl.BlockSpec((tm, D), lambda i, *_: (jnp.maximum(i - B * NB, 0), 0))],
            scratch_shapes=[pltpu.VMEM((2, tm, D), F32), pltpu.VMEM((2, tm, D), F32),
                            pltpu.SemaphoreType.DMA((2,))]),
        out_shape=[jax.ShapeDtypeStruct((B * SEQ, D), F32), jax.ShapeDtypeStruct((MS, D), F32)],
        compiler_params=_cp("arbitrary"),
        name="moe_combine",
    )(slot1, slot2, x1, route, g.reshape(1, D), ys)


def _moe_plan(route):
    e1 = route[:, 0].astype(jnp.int32)
    e2 = route[:, 1].astype(jnp.int32)
    ids = jnp.arange(NE, dtype=jnp.int32)
    sel = (e1[:, None] == ids).astype(jnp.int32) + (e2[:, None] == ids).astype(jnp.int32)
    incl = jnp.cumsum(sel, axis=0)
    rank = incl - sel
    cnt = incl[-1]
    ntile = (cnt + MOE_TM - 1) // MOE_TM
    tile_end = jnp.cumsum(ntile)
    tile_off = tile_end - ntile
    row_off = tile_off * MOE_TM
    slot1 = jnp.take(row_off, e1) + jnp.take_along_axis(rank, e1[:, None], axis=1)[:, 0]
    slot2 = jnp.take(row_off, e2) + jnp.take_along_axis(rank, e2[:, None], axis=1)[:, 0]
    tiles = jnp.arange(MOE_TILES, dtype=jnp.int32)
    tile_expert = jnp.minimum(jnp.sum((tiles[:, None] >= tile_end[None, :]).astype(jnp.int32), axis=1), NE - 1)
    n_used = tile_end[-1:].astype(jnp.int32)
    tail = jnp.minimum(tile_end[-1] + ids, MOE_TILES - 1)
    zero_tiles = jnp.concatenate([jnp.maximum(tile_end - 1, 0), tail]).astype(jnp.int32)
    return slot1.astype(jnp.int32), slot2.astype(jnp.int32), tile_expert.astype(jnp.int32), n_used, zero_tiles


def _blockdiag(w):
    per = GATE_CH // (LW // 16)
    w4 = w.reshape(LW // GATE_CH, per, 64, 64)
    eye = jnp.eye(per, dtype=w.dtype)
    return jnp.einsum("cnde,nm->cndme", w4, eye).reshape(LW // GATE_CH, GATE_CH, GATE_CH).astype(BF16)


def kernel(x_prompt, x_sample, cache_attn_k, cache_attn_v, state_conv, state_lru_conv, state_lru_h, meta_tokens,
           norm_mix, norm_ffn, norm_final, w_in, attn_sinks, w_attn_out, conv_dw_w, conv_dw_b, conv_ln_g, conv_ln_b,
           w_conv_out, lru_conv_w, lru_conv_b, lru_w_r, lru_b_r, lru_w_i, lru_b_i, lru_lambda, w_lru_out, w_mix_out,
           ffn_w1, ffn_w3, ffn_w2, moe_router_w, moe_router_b, moe_w1, moe_w3, moe_w2):
    states_in = (cache_attn_k.reshape(DEPTH * DB, WIN, KVW), cache_attn_v.reshape(DEPTH * DB, WIN, KVW),
                 state_conv.reshape(DEPTH * DB, CK - 1, CW), state_lru_conv.reshape(DEPTH * DB, LCK - 1, LW),
                 state_lru_h.reshape(DEPTH * DB, LW))
    states_out = tuple(jnp.zeros(s.shape, F32) for s in states_in)
    p_k, p_v, p_conv, p_lconv, p_h = [], [], [], [], []

    assert DEPTH == 2, "layer 0 dense SwiGLU, layer 1 routed experts followed by the final norm"
    w_in_b = w_in.astype(BF16)
    wa, wc, wl, wm = (w.astype(BF16) for w in (w_attn_out, w_conv_out, w_lru_out, w_mix_out))
    ffn_b = tuple(w.astype(BF16) for w in (ffn_w1, ffn_w3, ffn_w2))
    moe_side = {0: moe_w1.reshape(-1, DFF), 1: moe_w3.reshape(-1, DFF)}
    moe_b = {}

    x, xn = embed_norm(x_prompt, x_sample, meta_tokens, norm_mix[0])
    y_prompt = y_sample = None
    for l in range(DEPTH):
        z, side_b = inproj(xn, w_in_b, l, moe_side[l])
        moe_b[l] = side_b.reshape(moe_w1.shape)
        sinks = attn_sinks[l]

        conv_p = (conv_dw_w[l], conv_dw_b[l], conv_ln_g[l], conv_ln_b[l])
        lru_p = (lru_conv_w[l], lru_conv_b[l], _blockdiag(lru_w_r[l]), lru_b_r[l], _blockdiag(lru_w_i[l]),
                 lru_b_i[l], lru_lambda[l])
        o_p, c_p, rr_p, pc, plc, ph = prompt_branches(z, sinks, conv_p, lru_p)
        o_s, c_s, rr_s, states_out = sample_branches(z, sinks, conv_p, lru_p, states_in, states_out, l)
        o, c, rr = (o_p, o_s), (c_p, c_s), (rr_p, rr_s)
        kv_last = jnp.stack([lax.slice(z, ((b + 1) * PB - WIN, Z_K), ((b + 1) * PB, Z_K + 2 * KVW))
                             for b in range(B)]).astype(F32)
        p_k.append(kv_last[:, :, :KVW].reshape(B, WIN, NKV, HD))
        p_v.append(kv_last[:, :, KVW:].reshape(B, WIN, NKV, HD))
        p_conv.append(pc)
        p_lconv.append(plc)
        p_h.append(ph.reshape(B, LW))

        j = l // 2
        if l % 2 == 0:
            x1, hn = mix(o, c, rr, z, x, wa, wc, wl, wm, norm_ffn[l], l)
            h, w2_b = ffn_up(hn, ffn_b[0], ffn_b[1], j, moe_w2.reshape(-1, D))
            moe_b[2] = w2_b.reshape(moe_w2.shape)
            x, xn = ffn_down(h, ffn_b[2], x1, norm_mix[l + 1], j)
        else:
            rw = jnp.pad(moe_router_w[j], ((0, 0), (0, ROUTE_W - NE)))
            rw_hi = rw.astype(BF16)
            rw_lo = (rw - rw_hi.astype(F32)).astype(BF16)
            rb = jnp.pad(moe_router_b[j].reshape(1, NE), ((0, 0), (0, ROUTE_W - NE)))
            x1, route = mix(o, c, rr, z, x, wa, wc, wl, wm, norm_ffn[l], l,
                            router=(jnp.concatenate([rw_hi, rw_lo], axis=1), rb))
            slot1, slot2, tile_expert, n_used, zero_tiles = _moe_plan(route)
            xs = moe_scatter(x1, norm_ffn[l], slot1, slot2, zero_tiles)
            hs = moe_up(xs, moe_b[0], moe_b[1], tile_expert, n_used, j)
            ys = moe_down(hs, moe_b[2], tile_expert, n_used, j)
            y_prompt, y_sample = moe_combine(x1, route, norm_final, ys, slot1, slot2)

    y_prompt = y_prompt.reshape(B, SEQ, D)
    y_sample = y_sample.reshape(DB, T, D)
    s_k, s_v, s_conv, s_lconv, s_h = states_out
    st4 = lambda a: a.reshape(DEPTH, DB, WIN, NKV, HD)
    return (y_prompt, y_sample,
            jnp.stack(p_k), jnp.stack(p_v), jnp.stack(p_conv), jnp.stack(p_lconv), jnp.stack(p_h),
            st4(s_k), st4(s_v), s_conv.reshape(DEPTH, DB, CK - 1, CW), s_lconv.reshape(DEPTH, DB, LCK - 1, LW),
            s_h.reshape(DEPTH, DB, LW))
```

```python
import functools
import math

import jax
import jax.numpy as jnp
from jax import lax
from jax.experimental import pallas as pl
from jax.experimental.pallas import tpu as pltpu

F32 = jnp.float32
BF16 = jnp.bfloat16

D = 2048
B = 2
SEQ = 4096
DEPTH = 2
DB = 128
T = 8
PAST = 8192
N_META = 16
NH = 16
NKV = 4
G = NH // NKV
HD = 64
AW = NH * HD
KVW = NKV * HD
WIN = 128
BLK = 128
CW = D // 2
CK = 31
LW = D // 2
LCK = 4
LRU_C = 8.0
DFF = 5632
NE = 8
EPS = 1e-6
NEG = -1e30
IN_W = AW + 2 * KVW + 2 * CW + 2 * LW + 3 * D

PAD = (-N_META) % BLK
PB = PAD + N_META + SEQ
NB = PB // BLK
MP = B * PB
MS = DB * T
M = MP + MS

ZC = 512
NZC = IN_W // ZC
Z_Q, Z_A, Z_GATE, Z_LX, Z_LG, Z_GA, Z_K, Z_V = 0, 1024, 2048, 3072, 4096, 5120, 11264, 11520

SLOPES = tuple(2.0 ** (-8.0 * (h + 1.0) / NH) for h in range(NH))

TM_BIG = 1184
TM_MID = 592
TM_TOK = 256
SB = 8
MOE_TM = 256
MOE_SKIP = B * BLK
MOE_TILES = (2 * (M - MOE_SKIP)) // MOE_TM + NE
MOE_ROWS = MOE_TILES * MOE_TM
UP_TN = 1408
DOWN_TN = 1024

VMEM_LIMIT = 56 * 1024 * 1024


def _cp(*sem):
    return pltpu.CompilerParams(dimension_semantics=sem, vmem_limit_bytes=VMEM_LIMIT)


def _rms(x, g):
    return x * lax.rsqrt(jnp.mean(x * x, axis=-1, keepdims=True) + EPS) * g


def _sigmoid(x):
    return 1.0 / (1.0 + jnp.exp(-x))


def _silu(x):
    return x * _sigmoid(x)


def _prompt_block(i):
    ic = jnp.minimum(i, B * NB - 1)
    return (ic // NB) * (NB - 1) + jnp.maximum(ic % NB - 1, 0)


def _embed_norm_kernel(xp_ref, xs_ref, meta_ref, g_ref, x_ref, xn_ref):
    i = pl.program_id(0)
    head = jnp.concatenate([jnp.zeros((PAD, D), F32), meta_ref[...]], axis=0)
    x = jnp.where(i % NB == 0, head, xp_ref[...])
    x = jnp.where(i < B * NB, x, xs_ref[...])
    x_ref[...] = x
    xn_ref[...] = _rms(x, g_ref[...]).astype(xn_ref.dtype)


def embed_norm(x_prompt, x_sample, meta, g):
    tm = BLK
    return pl.pallas_call(
        _embed_norm_kernel,
        grid=(M // tm,),
        in_specs=[pl.BlockSpec((tm, D), lambda i: (_prompt_block(i), 0)),
                  pl.BlockSpec((tm, D), lambda i: (jnp.maximum(i - B * NB, 0), 0)),
                  pl.BlockSpec((N_META, D), lambda i: (0, 0)), pl.BlockSpec((1, D), lambda i: (0, 0))],
        out_specs=[pl.BlockSpec((tm, D), lambda i: (i, 0)), pl.BlockSpec((tm, D), lambda i: (i, 0))],
        out_shape=[jax.ShapeDtypeStruct((M, D), F32), jax.ShapeDtypeStruct((M, D), BF16)],
        compiler_params=_cp("parallel"),
        name="embed_norm",
    )(x_prompt.reshape(B * SEQ, D), x_sample.reshape(MS, D), meta, g.reshape(1, D))


def _side_plan(streams, step_of, total_steps):
    plan, in_specs, out_specs, out_shapes = [], [], [], []
    first = 0
    for arr, rows, lead in streams:
        nrows, ncols = arr.shape[-2:]
        nblk = nrows // rows
        assert nblk * rows == nrows

        def block(*g, first=first, nblk=nblk):
            return jnp.clip(step_of(*g) - first, 0, nblk - 1)

        if lead is None:
            in_specs.append(pl.BlockSpec((rows, ncols), lambda *g, block=block: (block(*g), 0)))
        else:
            in_specs.append(pl.BlockSpec((None, rows, ncols), lambda *g, block=block, lead=lead: (lead, block(*g), 0)))
        out_specs.append(pl.BlockSpec((rows, ncols), lambda *g, block=block: (block(*g), 0)))
        out_shapes.append(jax.ShapeDtypeStruct((nrows, ncols), BF16))
        plan.append((first, nblk))
        first += nblk
    assert first <= total_steps
    return plan, in_specs, out_specs, out_shapes


def _run_side(plan, src_refs, dst_refs, step):
    for (first, nblk), src_ref, dst_ref in zip(plan, src_refs, dst_refs):
        @pl.when((step >= first) & (step < first + nblk))
        def _():
            dst_ref[...] = src_ref[...].astype(dst_ref.dtype)


def _inproj_kernel(x_ref, w_ref, *refs, plan):
    n = len(plan)
    o_ref = refs[n]
    o_ref[...] = jnp.dot(x_ref[...], w_ref[...], preferred_element_type=F32).astype(o_ref.dtype)
    _run_side(plan, refs[:n], refs[n + 1:], pl.program_id(0) * NZC + pl.program_id(1))


def _z_src_block(j):
    return jnp.where(j < 2, j, jnp.where(j < NZC - 1, j + 1, 2))


def inproj(xn, w, streams):
    tm = TM_BIG
    steps = (M // tm) * NZC
    plan, s_in, s_out, s_shapes = _side_plan(streams, lambda i, j: i * NZC + j, steps)
    outs = pl.pallas_call(
        functools.partial(_inproj_kernel, plan=plan),
        grid=(M // tm, NZC),
        in_specs=[pl.BlockSpec((tm, D), lambda i, j: (i, 0)),
                  pl.BlockSpec((D, ZC), lambda i, j: (0, _z_src_block(j)))] + s_in,
        out_specs=[pl.BlockSpec((tm, ZC), lambda i, j: (i, j))] + s_out,
        out_shape=[jax.ShapeDtypeStruct((M, IN_W), BF16)] + s_shapes,
        compiler_params=_cp("arbitrary", "arbitrary"),
        name="inproj",
    )(xn, w, *[s[0] for s in streams])
    return outs[0], outs[1:]


def _softmax_sink(s, sink):
    m = jnp.maximum(jnp.max(s, axis=-1, keepdims=True), sink)
    e = jnp.exp(s - m)
    l = jnp.sum(e, axis=-1, keepdims=True) + jnp.exp(sink - m)
    return e / l


def _attn_prompt_start(n, bias_ref):
    @pl.when(n == 0)
    def _():
        r = lax.broadcasted_iota(jnp.int32, (BLK, 2 * BLK), 0)
        c = lax.broadcasted_iota(jnp.int32, (BLK, 2 * BLK), 1)
        dist = BLK + r - c
        in_window = (dist >= 0) & (dist <= WIN)
        distf = dist.astype(F32)
        for hq in range(NH):
            bias_ref[hq] = jnp.where(in_window, -SLOPES[hq] * distf, NEG)


def _attn_prompt_main(n, sink_ref, q_ref, kp_ref, kc_ref, vp_ref, vc_ref, o_ref, bias_ref):
    c = lax.broadcasted_iota(jnp.int32, (BLK, 2 * BLK), 1)
    exists = (n - 1) * BLK - PAD + c >= 0
    q = q_ref[...] * (HD ** -0.5)
    k = jnp.concatenate([kp_ref[...], kc_ref[...]], axis=0)
    v = jnp.concatenate([vp_ref[...], vc_ref[...]], axis=0)
    for h in range(NKV):
        kh = k[:, h * HD:(h + 1) * HD]
        vh = v[:, h * HD:(h + 1) * HD]
        qh = jnp.concatenate([q[:, (h * G + g) * HD:(h * G + g + 1) * HD] for g in range(G)], axis=0)
        s = lax.dot_general(qh, kh, (((1,), (1,)), ((), ())), preferred_element_type=F32)
        ps = []
        for g in range(G):
            hq = h * G + g
            sg = jnp.where(exists, s[g * BLK:(g + 1) * BLK] + bias_ref[hq], NEG)
            ps.append(_softmax_sink(sg, sink_ref[hq]).astype(BF16))
        oh = jnp.dot(jnp.concatenate(ps, axis=0), vh, preferred_element_type=F32)
        for g in range(G):
            hq = h * G + g
            o_ref[:, hq * HD:(hq + 1) * HD] = oh[g * BLK:(g + 1) * BLK].astype(o_ref.dtype)


def _attn_sample_body(sink_ref, q_ref, kn_ref, vn_ref, ck_ref, cv_ref, o_ref, nk_ref, nv_ref):
    q = q_ref[...].astype(F32).reshape(SB, T, AW)
    kn = kn_ref[...].astype(F32).reshape(SB, T, KVW)
    vn = vn_ref[...].astype(F32).reshape(SB, T, KVW)
    ck = ck_ref[...]
    cv = cv_ref[...]
    wb = WIN
    nk_ref[:, 0:wb - T, :] = ck[:, T:wb, :]
    nk_ref[:, wb - T:wb, :] = kn
    nv_ref[:, 0:wb - T, :] = cv[:, T:wb, :]
    nv_ref[:, wb - T:wb, :] = vn
    zpad = jnp.zeros((SB, wb - T, KVW), F32)
    k = jnp.concatenate([ck, kn, zpad], axis=1).astype(BF16)
    v = jnp.concatenate([cv, vn, zpad], axis=1).astype(BF16)
    nkeys = 2 * wb
    r = lax.broadcasted_iota(jnp.int32, (G * T, nkeys), 0)
    c = lax.broadcasted_iota(jnp.int32, (G * T, nkeys), 1)
    dist = wb + (r % T) - c
    allowed = (dist >= 0) & (dist <= WIN)
    distf = dist.astype(F32)
    gidx = r // T
    for h in range(NKV):
        slope = jnp.zeros((G * T, nkeys), F32)
        sink = jnp.zeros((G * T, 1), F32)
        for g in range(G):
            slope = jnp.where(gidx == g, SLOPES[h * G + g], slope)
            sink = jnp.where(gidx[:, 0:1] == g, sink_ref[h * G + g], sink)
        kh = k[:, :, h * HD:(h + 1) * HD]
        vh = v[:, :, h * HD:(h + 1) * HD]
        qh = jnp.concatenate([q[:, :, (h * G + g) * HD:(h * G + g + 1) * HD] for g in range(G)],
                             axis=1).astype(BF16)
        s = jnp.einsum("bqd,bkd->bqk", qh, kh, preferred_element_type=F32)
        s = s * (HD ** -0.5) - (slope * distf)[None]
        s = jnp.where(allowed[None], s, NEG)
        p = _softmax_sink(s, sink[None]).astype(BF16)
        oh = jnp.einsum("bqk,bkd->bqd", p, vh, preferred_element_type=F32)
        for g in range(G):
            hq = h * G + g
            piece = oh[:, g * T:(g + 1) * T, :].reshape(SB * T, HD)
            o_ref[:, hq * HD:(hq + 1) * HD] = piece.astype(o_ref.dtype)


def _ln_swish(y, g, b):
    mu = jnp.mean(y, axis=-1, keepdims=True)
    yc = y - mu
    var = jnp.mean(yc * yc, axis=-1, keepdims=True)
    yn = yc * lax.rsqrt(var + EPS) * g + b
    return _silu(yn)


CONV_HIST = 32
SUBLANES = 8


def _dwconv_by_phase(load, wrow, nrows, first, ntaps):
    y = None
    for s in range(SUBLANES):
        n = nrows if s == 0 else nrows + SUBLANES
        part = None
        for q in range((first + ntaps - 1) // SUBLANES + 1):
            o = SUBLANES * q + s
            if first <= o < first + ntaps:
                term = wrow(o - first) * load(SUBLANES * q, n)
                part = term if part is None else part + term
        if part is not None:
            part = part[s:s + nrows]
            y = part if y is None else y + part
    return y


def _conv_prompt_start(n, ext_ref):
    @pl.when(n == 0)
    def _():
        ext_ref[0:CONV_HIST, :] = jnp.zeros((CONV_HIST, CW), F32)

    @pl.when(n > 0)
    def _():
        ext_ref[0:CONV_HIST, :] = ext_ref[BLK:BLK + CONV_HIST, :]


def _conv_prompt_main(n, a_ref, g_ref, w_ref, b_ref, lg_ref, lb_ref, c_ref, ext_ref, y_ref):
    u = a_ref[...].astype(F32) * _sigmoid(g_ref[...].astype(F32))
    row = lax.broadcasted_iota(jnp.int32, (BLK, 1), 0)
    u = jnp.where((n > 0) | (row >= PAD), u, 0.0)
    ext_ref[CONV_HIST:CONV_HIST + BLK, :] = u
    off = CONV_HIST - (CK - 1)
    cc = 256
    for c0 in range(0, CW, cc):
        cs = slice(c0, c0 + cc)
        y_ref[:, cs] = b_ref[:, cs] + _dwconv_by_phase(lambda r0, nr: ext_ref[r0:r0 + nr, cs],
                                                      lambda j: w_ref[j:j + 1, cs], BLK, off, CK)
    c_ref[...] = _ln_swish(y_ref[...], lg_ref[...], lb_ref[...]).astype(c_ref.dtype)


def _conv_prompt_finish(n, st_ref, ext_ref):
    @pl.when(n == NB - 1)
    def _():
        st_ref[0] = ext_ref[CONV_HIST + BLK - (CK - 1):CONV_HIST + BLK, :]


def _conv_sample_body(a_ref, g_ref, st_ref, w_ref, b_ref, lg_ref, lb_ref, c_ref, nst_ref, ext_ref, y_ref):
    u = a_ref[...].astype(F32) * _sigmoid(g_ref[...].astype(F32))
    hist = CK - 1
    base = CONV_HIST - hist
    for s in range(SB):
        ext_ref[s, 0:base, :] = jnp.zeros((base, CW), F32)
        ext_ref[s, base:CONV_HIST, :] = st_ref[s]
        ext_ref[s, CONV_HIST:CONV_HIST + T, :] = u[s * T:(s + 1) * T]
    for s in range(SB):
        y_ref[s * T:(s + 1) * T, :] = b_ref[...] + _dwconv_by_phase(
            lambda r0, nr: ext_ref[s, r0:r0 + nr, :], lambda j: w_ref[j:j + 1, :], T, base, CK)
        nst_ref[s] = ext_ref[s, base + T:base + T + hist, :]
    c_ref[...] = _ln_swish(y_ref[...], lg_ref[...], lb_ref[...]).astype(c_ref.dtype)


LRU_HIST = 8
GATE_CH = 256


def _gelu_tanh(x):
    return x * (0.5 * (1.0 + jnp.tanh(math.sqrt(2.0 / math.pi) * (x + 0.044715 * (x * x * x)))))


def _softplus(x):
    return jnp.maximum(x, 0.0) + jnp.log1p(jnp.exp(-jnp.abs(x)))


def _expm1(x):
    return jnp.tanh(0.5 * x) * (jnp.exp(x) + 1.0)


def _lru_gates(xc, wr_ref, br_ref, wi_ref, bi_ref, lam_ref, valid, a_ref, bx_ref):
    sp = _softplus(-lam_ref[...])
    for k in range(LW // GATE_CH):
        sl = slice(k * GATE_CH, (k + 1) * GATE_CH)
        xk = xc[:, sl]
        xkb = xk.astype(BF16)
        r = _sigmoid(jnp.dot(xkb, wr_ref[k], preferred_element_type=F32) + br_ref[:, sl])
        i = _sigmoid(jnp.dot(xkb, wi_ref[k], preferred_element_type=F32) + bi_ref[:, sl])
        log_a = (-LRU_C) * r * sp[:, sl]
        a = jnp.exp(log_a)
        bx = jnp.sqrt(-_expm1(2.0 * log_a)) * (i * xk)
        if valid is not None:
            bx = jnp.where(valid, bx, 0.0)
        a_ref[:, sl] = a
        bx_ref[:, sl] = bx


def _lru_prompt_start(n, ext_ref, carry_ref):
    @pl.when(n == 0)
    def _():
        ext_ref[0:LRU_HIST, :] = jnp.zeros((LRU_HIST, LW), F32)
        carry_ref[...] = jnp.zeros((1, LW), F32)

    @pl.when(n > 0)
    def _():
        ext_ref[0:LRU_HIST, :] = ext_ref[BLK:BLK + LRU_HIST, :]


def _lru_prompt_main(n, x_ref, g_ref, cw_ref, cb_ref, wr_ref, br_ref, wi_ref, bi_ref, lam_ref,
                     rr_ref, ext_ref, a_ref, bx_ref, h_ref, carry_ref):
    row = lax.broadcasted_iota(jnp.int32, (BLK, 1), 0)
    valid = (n > 0) | (row >= PAD)
    ext_ref[LRU_HIST:LRU_HIST + BLK, :] = jnp.where(valid, x_ref[...].astype(F32), 0.0)
    off = LRU_HIST - (LCK - 1)
    xc = jnp.zeros((BLK, LW), F32) + cb_ref[...]
    for j in range(LCK):
        xc = xc + cw_ref[j:j + 1, :] * ext_ref[off + j:off + j + BLK, :]
    _lru_gates(xc, wr_ref, br_ref, wi_ref, bi_ref, lam_ref, valid, a_ref, bx_ref)

    h = carry_ref[...]
    for t in range(BLK):
        h = a_ref[t:t + 1, :] * h + bx_ref[t:t + 1, :]
        h_ref[t:t + 1, :] = h
    carry_ref[...] = h
    rr_ref[...] = (h_ref[...] * _gelu_tanh(g_ref[...].astype(F32))).astype(rr_ref.dtype)


def _lru_prompt_finish(n, cst_ref, hst_ref, ext_ref, carry_ref):
    @pl.when(n == NB - 1)
    def _():
        cst_ref[0] = ext_ref[LRU_HIST + BLK - (LCK - 1):LRU_HIST + BLK, :]
        hst_ref[0] = carry_ref[...]


def _prompt_branches_kernel(sink_ref, q_ref, kp_ref, kc_ref, vp_ref, vc_ref, ca_ref, cg_ref, lx_ref, lg_ref,
                            dw_ref, db_ref, lng_ref, lnb_ref,
                            cw_ref, cb_ref, wr_ref, br_ref, wi_ref, bi_ref, lam_ref,
                            o_ref, c_ref, rr_ref, cst_ref, lcst_ref, hst_ref,
                            bias_ref, cext_ref, cy_ref, lext_ref, a_ref, bx_ref, h_ref, carry_ref):
    n = pl.program_id(1)
    _attn_prompt_start(n, bias_ref)
    _conv_prompt_start(n, cext_ref)
    _lru_prompt_start(n, lext_ref, carry_ref)
    _attn_prompt_main(n, sink_ref, q_ref, kp_ref, kc_ref, vp_ref, vc_ref, o_ref, bias_ref)
    _conv_prompt_main(n, ca_ref, cg_ref, dw_ref, db_ref, lng_ref, lnb_ref, c_ref, cext_ref, cy_ref)
    _lru_prompt_main(n, lx_ref, lg_ref, cw_ref, cb_ref, wr_ref, br_ref, wi_ref, bi_ref, lam_ref,
                     rr_ref, lext_ref, a_ref, bx_ref, h_ref, carry_ref)
    _conv_prompt_finish(n, cst_ref, cext_ref)
    _lru_prompt_finish(n, lcst_ref, hst_ref, lext_ref, carry_ref)


def prompt_branches(z, sinks, conv_p, lru_p):
    kb, vb = Z_K // KVW, Z_V // KVW
    cur = lambda col: (lambda b, n: (b * NB + n, col))
    prev = lambda col: (lambda b, n: (b * NB + jnp.maximum(n - 1, 0), col))
    full = lambda shape: pl.BlockSpec(shape, lambda b, n: (0,) * len(shape))
    wide = lambda col: pl.BlockSpec((BLK, 1024), cur(col))
    vec = full((1, 1024))
    wspec = full((LW // GATE_CH, GATE_CH, GATE_CH))
    dw_w, dw_b, ln_g, ln_b = conv_p
    cw, cb, wr, br, wi, bi, lam = lru_p
    row = lambda v: v.reshape(1, -1)
    blk = pltpu.VMEM((BLK, 1024), F32)
    return pl.pallas_call(
        _prompt_branches_kernel,
        grid=(B, NB),
        in_specs=[pl.BlockSpec(memory_space=pltpu.SMEM),
                  wide(0),
                  pl.BlockSpec((BLK, KVW), prev(kb)), pl.BlockSpec((BLK, KVW), cur(kb)),
                  pl.BlockSpec((BLK, KVW), prev(vb)), pl.BlockSpec((BLK, KVW), cur(vb)),
                  wide(Z_A // 1024), wide(Z_GATE // 1024), wide(Z_LX // 1024), wide(Z_LG // 1024),
                  full((CK, CW)), vec, vec, vec,
                  full((LCK, LW)), vec, wspec, vec, wspec, vec, vec],
        out_specs=[wide(0), wide(0), wide(0),
                   pl.BlockSpec((1, CK - 1, CW), lambda b, n: (b, 0, 0)),
                   pl.BlockSpec((1, LCK - 1, LW), lambda b, n: (b, 0, 0)),
                   pl.BlockSpec((1, 1, LW), lambda b, n: (b, 0, 0))],
        out_shape=[jax.ShapeDtypeStruct((MP, AW), BF16), jax.ShapeDtypeStruct((MP, CW), BF16),
                   jax.ShapeDtypeStruct((MP, LW), BF16), jax.ShapeDtypeStruct((B, CK - 1, CW), F32),
                   jax.ShapeDtypeStruct((B, LCK - 1, LW), F32), jax.ShapeDtypeStruct((B, 1, LW), F32)],
        scratch_shapes=[pltpu.VMEM((NH, BLK, 2 * BLK), F32),
                        pltpu.VMEM((CONV_HIST + BLK, CW), F32), blk,
                        pltpu.VMEM((LRU_HIST + BLK, LW), F32), blk, blk, blk, pltpu.VMEM((1, LW), F32)],
        compiler_params=_cp("parallel", "arbitrary"),
        name="prompt_branches",
    )(sinks, z, z, z, z, z, z, z, z, z, dw_w, row(dw_b), row(ln_g), row(ln_b),
      cw, row(cb), wr, row(br), wi, row(bi), row(lam))


def _lru_sample_body(x_ref, g_ref, cst_ref, h0_ref, cw_ref, cb_ref, wr_ref, br_ref, wi_ref, bi_ref, lam_ref,
                     rr_ref, ncst_ref, nh_ref, ext_ref, xc_ref, a_ref, bx_ref, h_ref):
    hist = LCK - 1
    base = LRU_HIST - hist
    x = x_ref[...].astype(F32)
    for s in range(SB):
        ext_ref[s, base:LRU_HIST, :] = cst_ref[s]
        ext_ref[s, LRU_HIST:LRU_HIST + T, :] = x[s * T:(s + 1) * T]
    for s in range(SB):
        acc = jnp.zeros((T, LW), F32) + cb_ref[...]
        for j in range(LCK):
            acc = acc + cw_ref[j:j + 1, :] * ext_ref[s, base + j:base + j + T, :]
        xc_ref[s * T:(s + 1) * T, :] = acc
        ncst_ref[s] = ext_ref[s, base + T:base + T + hist, :]
    _lru_gates(xc_ref[...], wr_ref, br_ref, wi_ref, bi_ref, lam_ref, None, a_ref, bx_ref)
    for s in range(SB):
        h = h0_ref[s:s + 1, :]
        for t in range(T):
            rw = s * T + t
            h = a_ref[rw:rw + 1, :] * h + bx_ref[rw:rw + 1, :]
            h_ref[rw:rw + 1, :] = h
        nh_ref[s:s + 1, :] = h
    rr_ref[...] = (h_ref[...] * _gelu_tanh(g_ref[...].astype(F32))).astype(rr_ref.dtype)


def _sample_branches_kernel(sink_ref, q_ref, kn_ref, vn_ref, ck_ref, cv_ref, ca_ref, cg_ref, cst_ref,
                            dw_ref, db_ref, lng_ref, lnb_ref, lx_ref, lg_ref, lcst_ref, h0_ref,
                            cw_ref, cb_ref, wr_ref, br_ref, wi_ref, bi_ref, lam_ref,
                            nk_in, nv_in, nst_in, ncst_in, nh_in,
                            o_ref, nk_ref, nv_ref, c_ref, nst_ref, rr_ref, ncst_ref, nh_ref,
                            cext_ref, cy_ref, lext_ref, xc_ref, a_ref, bx_ref, h_ref):
    del nk_in, nv_in, nst_in, ncst_in, nh_in
    _attn_sample_body(sink_ref, q_ref, kn_ref, vn_ref, ck_ref, cv_ref, o_ref, nk_ref, nv_ref)
    _conv_sample_body(ca_ref, cg_ref, cst_ref, dw_ref, db_ref, lng_ref, lnb_ref, c_ref, nst_ref, cext_ref, cy_ref)
    _lru_sample_body(lx_ref, lg_ref, lcst_ref, h0_ref, cw_ref, cb_ref, wr_ref, br_ref, wi_ref, bi_ref, lam_ref,
                     rr_ref, ncst_ref, nh_ref, lext_ref, xc_ref, a_ref, bx_ref, h_ref)


def sample_branches(z, sinks, conv_p, lru_p, states_in, states_out, layer):
    rows = SB * T
    rb0 = MP // rows
    sb0 = layer * (DB // SB)
    kb, vb = Z_K // KVW, Z_V // KVW
    zrow = lambda w, col: pl.BlockSpec((rows, w), lambda i: (rb0 + i, col))
    full = lambda shape: pl.BlockSpec(shape, lambda i: (0,) * len(shape))
    st3 = lambda n, w: pl.BlockSpec((SB, n, w), lambda i: (sb0 + i, 0, 0))
    vec = full((1, 1024))
    wspec = full((LW // GATE_CH, GATE_CH, GATE_CH))
    kvspec, cspec, lcspec = st3(WIN, KVW), st3(CK - 1, CW), st3(LCK - 1, LW)
    hspec = pl.BlockSpec((SB, LW), lambda i: (sb0 + i, 0))
    anyspec = pl.BlockSpec(memory_space=pl.ANY)
    out_row = pl.BlockSpec((rows, 1024), lambda i: (i, 0))
    dw_w, dw_b, ln_g, ln_b = conv_p
    cw, cb, wr, br, wi, bi, lam = lru_p
    row = lambda v: v.reshape(1, -1)
    blk = pltpu.VMEM((rows, 1024), F32)
    shape_of = lambda a: jax.ShapeDtypeStruct(a.shape, a.dtype)
    n_in = 24
    outs = pl.pallas_call(
        _sample_branches_kernel,
        grid=(DB // SB,),
        in_specs=[pl.BlockSpec(memory_space=pltpu.SMEM),
                  zrow(AW, 0), zrow(KVW, kb), zrow(KVW, vb), kvspec, kvspec,
                  zrow(CW, Z_A // CW), zrow(CW, Z_GATE // CW), cspec, full((CK, CW)), vec, vec, vec,
                  zrow(LW, Z_LX // LW), zrow(LW, Z_LG // LW), lcspec, hspec,
                  full((LCK, LW)), vec, wspec, vec, wspec, vec, vec] + [anyspec] * 5,
        out_specs=[out_row, kvspec, kvspec, out_row, cspec, out_row, lcspec, hspec],
        out_shape=[jax.ShapeDtypeStruct((MS, AW), BF16), shape_of(states_out[0]), shape_of(states_out[1]),
                   jax.ShapeDtypeStruct((MS, CW), BF16), shape_of(states_out[2]),
                   jax.ShapeDtypeStruct((MS, LW), BF16), shape_of(states_out[3]), shape_of(states_out[4])],
        scratch_shapes=[pltpu.VMEM((SB, CONV_HIST + T, CW), F32), blk,
                        pltpu.VMEM((SB, LRU_HIST + T, LW), F32), blk, blk, blk, blk],
        input_output_aliases={n_in: 1, n_in + 1: 2, n_in + 2: 4, n_in + 3: 6, n_in + 4: 7},
        compiler_params=_cp("parallel"),
        name="sample_branches",
    )(sinks, z, z, z, states_in[0], states_in[1], z, z, states_in[2], dw_w, row(dw_b), row(ln_g), row(ln_b),
      z, z, states_in[3], states_in[4], cw, row(cb), wr, row(br), wi, row(bi), row(lam), *states_out)
    o, nk, nv, c, nst, rr, ncst, nh = outs
    return o, c, rr, (nk, nv, nst, ncst, nh)


ROUTE_W = 128


def _mix_body(branch_refs, gates, x_ref, wa_ref, wc_ref, wl_ref, wm_ref, g_ref, x1_ref):
    is_prompt = pl.program_id(0) < MP // TM_TOK
    o, c, r = (jnp.where(is_prompt, p_ref[...], s_ref[...]) for p_ref, s_ref in branch_refs)
    half = D // 2
    parts = []
    for hh in range(2):
        sl = slice(hh * half, (hh + 1) * half)
        m = _sigmoid(gates[0][hh][...].astype(F32)) * jnp.dot(o, wa_ref[:, sl], preferred_element_type=F32)
        m = m + _sigmoid(gates[1][hh][...].astype(F32)) * jnp.dot(c, wc_ref[:, sl], preferred_element_type=F32)
        m = m + _sigmoid(gates[2][hh][...].astype(F32)) * jnp.dot(r, wl_ref[:, sl], preferred_element_type=F32)
        parts.append(m.astype(BF16))
    merged = jnp.concatenate(parts, axis=1)
    x1 = x_ref[...] + jnp.dot(merged, wm_ref[...], preferred_element_type=F32)
    x1_ref[...] = x1
    return _rms(x1, g_ref[...])


def _mix_kernel(op_ref, os_ref, cp_ref, cs_ref, rp_ref, rs_ref, ga0, ga1, gb0, gb1, gc0, gc1, x_ref,
                wa_ref, wc_ref, wl_ref, wm_ref, g_ref, x1_ref, hn_ref):
    hn = _mix_body(((op_ref, os_ref), (cp_ref, cs_ref), (rp_ref, rs_ref)), ((ga0, ga1), (gb0, gb1), (gc0, gc1)),
                   x_ref, wa_ref, wc_ref, wl_ref, wm_ref, g_ref, x1_ref)
    hn_ref[...] = hn.astype(hn_ref.dtype)


def _mix_router_kernel(op_ref, os_ref, cp_ref, cs_ref, rp_ref, rs_ref, ga0, ga1, gb0, gb1, gc0, gc1, x_ref,
                       wa_ref, wc_ref, wl_ref, wm_ref, g_ref, rw_ref, rb_ref, x1_ref, route_ref):
    hn = _mix_body(((op_ref, os_ref), (cp_ref, cs_ref), (rp_ref, rs_ref)), ((ga0, ga1), (gb0, gb1), (gc0, gc1)),
                   x_ref, wa_ref, wc_ref, wl_ref, wm_ref, g_ref, x1_ref)
    h_hi = hn.astype(BF16)
    h_lo = (hn - h_hi.astype(F32)).astype(BF16)
    both = jnp.dot(h_hi, rw_ref[...], preferred_element_type=F32)
    logits = (both[:, :ROUTE_W] + both[:, ROUTE_W:]
              + jnp.dot(h_lo, rw_ref[:, :ROUTE_W], preferred_element_type=F32) + rb_ref[...])
    lane = lax.broadcasted_iota(jnp.int32, logits.shape, 1)
    ninf = -jnp.inf
    l1 = jnp.where(lane < NE, logits, ninf)
    m1 = jnp.max(l1, axis=-1, keepdims=True)
    i1 = jnp.min(jnp.where(l1 == m1, lane, ROUTE_W), axis=-1, keepdims=True)
    l2 = jnp.where(lane == i1, ninf, l1)
    m2 = jnp.max(l2, axis=-1, keepdims=True)
    i2 = jnp.min(jnp.where(l2 == m2, lane, ROUTE_W), axis=-1, keepdims=True)
    e2 = jnp.exp(m2 - m1)
    den = 1.0 + e2
    out = jnp.where(lane == 0, i1.astype(F32), 0.0)
    out = jnp.where(lane == 1, i2.astype(F32), out)
    out = jnp.where(lane == 2, 1.0 / den, out)
    out = jnp.where(lane == 3, e2 / den, out)
    route_ref[...] = out


def mix(o, c, rr, z, x, wa, wc, wl, wm, g, layer, router=None):
    tm = TM_TOK
    half = D // 2
    npt = MP // tm
    row = lambda col: (lambda i: (i, col))
    const = lambda shape: pl.BlockSpec(shape, lambda i: (0,) * len(shape), pipeline_mode=pl.Buffered(1))
    wspec = lambda k: pl.BlockSpec((None, k, D), lambda i: (layer, 0, 0), pipeline_mode=pl.Buffered(1))
    gate_specs = [pl.BlockSpec((tm, half), row(Z_GA // half + k)) for k in range(6)]
    pair = lambda w: [pl.BlockSpec((tm, w), lambda i: (jnp.minimum(i, npt - 1), 0)),
                      pl.BlockSpec((tm, w), lambda i: (jnp.maximum(i - npt, 0), 0))]
    in_specs = (pair(AW) + pair(CW) + pair(LW) + gate_specs
                + [pl.BlockSpec((tm, D), row(0)), wspec(AW), wspec(CW), wspec(LW), wspec(D), const((1, D))])
    args = [*o, *c, *rr, z, z, z, z, z, z, x, wa, wc, wl, wm, g.reshape(1, D)]
    if router is None:
        kern = _mix_kernel
        out_specs = [pl.BlockSpec((tm, D), row(0)), pl.BlockSpec((tm, D), row(0))]
        out_shape = [jax.ShapeDtypeStruct((M, D), F32), jax.ShapeDtypeStruct((M, D), BF16)]
    else:
        kern = _mix_router_kernel
        rw, rb = router
        in_specs += [const((D, 2 * ROUTE_W)), const((1, ROUTE_W))]
        args += [rw, rb]
        out_specs = [pl.BlockSpec((tm, D), row(0)), pl.BlockSpec((tm, ROUTE_W), row(0))]
        out_shape = [jax.ShapeDtypeStruct((M, D), F32), jax.ShapeDtypeStruct((M, ROUTE_W), F32)]
    return pl.pallas_call(
        kern,
        grid=(M // tm,),
        in_specs=in_specs,
        out_specs=out_specs,
        out_shape=out_shape,
        compiler_params=_cp("parallel"),
        name="mix",
    )(*args)


def _ffn_up_kernel(x_ref, w1_ref, w3_ref, *refs, plan):
    n = len(plan)
    o_ref = refs[n]
    x = x_ref[...]
    a = jnp.dot(x, w1_ref[...], preferred_element_type=F32)
    b = jnp.dot(x, w3_ref[...], preferred_element_type=F32)
    o_ref[...] = (_silu(a) * b).astype(o_ref.dtype)
    _run_side(plan, refs[:n], refs[n + 1:], pl.program_id(0) * pl.num_programs(1) + pl.program_id(1))


def ffn_up(hn, w1, w3, streams):
    tm, tn = TM_BIG, 512
    nj = DFF // tn
    wspec = pl.BlockSpec((D, tn), lambda i, j: (0, j))
    plan, s_in, s_out, s_shapes = _side_plan(streams, lambda i, j: i * nj + j, (M // tm) * nj)
    outs = pl.pallas_call(
        functools.partial(_ffn_up_kernel, plan=plan),
        grid=(M // tm, nj),
        in_specs=[pl.BlockSpec((tm, D), lambda i, j: (i, 0)), wspec, wspec] + s_in,
        out_specs=[pl.BlockSpec((tm, tn), lambda i, j: (i, j))] + s_out,
        out_shape=[jax.ShapeDtypeStruct((M, DFF), BF16)] + s_shapes,
        compiler_params=_cp("arbitrary", "arbitrary"),
        name="ffn_up",
    )(hn, w1, w3, *[s[0] for s in streams])
    return outs[0], outs[1:]


def _ffn_down_kernel(h_ref, w_ref, x_ref, g_ref, *refs, plan):
    n = len(plan)
    x2_ref, xn_ref = refs[n], refs[n + 1]
    x2 = x_ref[...] + jnp.dot(h_ref[...], w_ref[...], preferred_element_type=F32)
    x2_ref[...] = x2
    xn_ref[...] = _rms(x2, g_ref[...]).astype(xn_ref.dtype)
    _run_side(plan, refs[:n], refs[n + 2:], pl.program_id(0))


def ffn_down(h, w2, x1, g, streams):
    tm = TM_TOK
    plan, s_in, s_out, s_shapes = _side_plan(streams, lambda i: i, M // tm)
    outs = pl.pallas_call(
        functools.partial(_ffn_down_kernel, plan=plan),
        grid=(M // tm,),
        in_specs=[pl.BlockSpec((tm, DFF), lambda i: (i, 0)),
                  pl.BlockSpec((DFF, D), lambda i: (0, 0), pipeline_mode=pl.Buffered(1)),
                  pl.BlockSpec((tm, D), lambda i: (i, 0)), pl.BlockSpec((1, D), lambda i: (0, 0))] + s_in,
        out_specs=[pl.BlockSpec((tm, D), lambda i: (i, 0)), pl.BlockSpec((tm, D), lambda i: (i, 0))] + s_out,
        out_shape=[jax.ShapeDtypeStruct((M, D), F32), jax.ShapeDtypeStruct((M, D), BF16)] + s_shapes,
        compiler_params=_cp("arbitrary"),
        name="ffn_down",
    )(h, w2, x1, g.reshape(1, D), *[s[0] for s in streams])
    return outs[0], outs[1], outs[2:]


def _row_copy(src, src_row, dst, dst_row, sem):
    return pltpu.make_async_copy(src.at[pl.ds(src_row, 1)], dst.at[pl.ds(dst_row, 1)], sem)


def _moe_scatter_kernel(s1_ref, s2_ref, zt_ref, x_ref, g_ref, xs_ref, buf_ref, zero_ref, sem):
    i = pl.program_id(0)
    tm = MOE_TM
    slot = i % 2

    def tile_copy(e):
        return pltpu.make_async_copy(zero_ref, xs_ref.at[pl.ds(pl.multiple_of(zt_ref[e] * tm, tm), tm)],
                                     sem.at[0])

    @pl.when(i == 0)
    def _():
        zero_ref[...] = jnp.zeros(zero_ref.shape, F32)
        for e in range(2 * NE):
            tile_copy(e).start()
            tile_copy(e).wait()

    buf_ref[slot] = _rms(x_ref[...], g_ref[...])

    def issue(r, carry):
        t = i * tm + r
        _row_copy(buf_ref.at[slot], r, xs_ref, s1_ref[t], sem.at[slot]).start()
        _row_copy(buf_ref.at[slot], r, xs_ref, s2_ref[t], sem.at[slot]).start()
        return carry

    lax.fori_loop(0, tm, issue, 0, unroll=8)

    def drain_slot(sl):
        def drain(r, carry):
            _row_copy(buf_ref.at[sl], 0, xs_ref, 0, sem.at[sl]).wait()
            _row_copy(buf_ref.at[sl], 0, xs_ref, 0, sem.at[sl]).wait()
            return carry

        lax.fori_loop(0, tm, drain, 0, unroll=8)

    @pl.when(i > 0)
    def _():
        drain_slot(1 - slot)

    @pl.when(i == pl.num_programs(0) - 1)
    def _():
        drain_slot(slot)


def moe_scatter(x1, g, slot1, slot2, zero_tiles):
    tm = MOE_TM
    return pl.pallas_call(
        _moe_scatter_kernel,
        grid_spec=pltpu.PrefetchScalarGridSpec(
            num_scalar_prefetch=3,
            grid=(M // tm,),
            in_specs=[pl.BlockSpec((tm, D), lambda i, *_: (i, 0)), pl.BlockSpec((1, D), lambda i, *_: (0, 0))],
            out_specs=pl.BlockSpec(memory_space=pl.ANY),
            scratch_shapes=[pltpu.VMEM((2, tm, D), F32), pltpu.VMEM((tm, D), F32),
                            pltpu.SemaphoreType.DMA((2,))]),
        out_shape=jax.ShapeDtypeStruct((MOE_ROWS + 2 * MOE_SKIP, D), F32),
        compiler_params=_cp("arbitrary"),
        name="moe_scatter",
    )(slot1, slot2, zero_tiles, x1, g.reshape(1, D))


def _moe_up_kernel(te_ref, nu_ref, x_ref, w1_ref, w3_ref, o_ref):
    used = pl.program_id(1) < nu_ref[0]

    @pl.when(used)
    def _():
        x = x_ref[...].astype(BF16)
        a = jnp.dot(x, w1_ref[0], preferred_element_type=F32)
        b = jnp.dot(x, w3_ref[0], preferred_element_type=F32)
        o_ref[...] = (_silu(a) * b).astype(o_ref.dtype)

    @pl.when(jnp.logical_not(used))
    def _():
        o_ref[...] = jnp.zeros(o_ref.shape, o_ref.dtype)


def _used(i, nu_ref):
    return jnp.minimum(i, nu_ref[0] - 1)


def moe_up(xs, w1, w3, tile_expert, n_used, idx):
    tm, tn = MOE_TM, UP_TN
    wspec = pl.BlockSpec((None, 1, D, tn), lambda j, i, te, nu: (idx, te[_used(i, nu)], 0, j))
    return pl.pallas_call(
        _moe_up_kernel,
        grid_spec=pltpu.PrefetchScalarGridSpec(
            num_scalar_prefetch=2,
            grid=(DFF // tn, MOE_TILES),
            in_specs=[pl.BlockSpec((tm, D), lambda j, i, te, nu: (_used(i, nu), 0)), wspec, wspec],
            out_specs=pl.BlockSpec((tm, tn), lambda j, i, te, nu: (i, j))),
        out_shape=jax.ShapeDtypeStruct((MOE_ROWS, DFF), BF16),
        compiler_params=_cp("arbitrary", "arbitrary"),
        name="moe_up",
    )(tile_expert, n_used, xs, w1, w3)


def _moe_down_kernel(te_ref, nu_ref, h_ref, w_ref, o_ref):
    used = pl.program_id(1) < nu_ref[0]

    @pl.when(used)
    def _():
        o_ref[...] = jnp.dot(h_ref[...], w_ref[0], preferred_element_type=F32)

    @pl.when(jnp.logical_not(used))
    def _():
        o_ref[...] = jnp.zeros(o_ref.shape, o_ref.dtype)


def moe_down(hs, w2, tile_expert, n_used, idx):
    tm, tn = MOE_TM, DOWN_TN
    return pl.pallas_call(
        _moe_down_kernel,
        grid_spec=pltpu.PrefetchScalarGridSpec(
            num_scalar_prefetch=2,
            grid=(D // tn, MOE_TILES),
            in_specs=[pl.BlockSpec((tm, DFF), lambda j, i, te, nu: (_used(i, nu), 0)),
                      pl.BlockSpec((None, 1, DFF, tn), lambda j, i, te, nu: (idx, te[_used(i, nu)], 0, j))],
            out_specs=pl.BlockSpec((tm, tn), lambda j, i, te, nu: (i, j))),
        out_shape=jax.ShapeDtypeStruct((MOE_ROWS, D), F32),
        compiler_params=_cp("arbitrary", "arbitrary"),
        name="moe_down",
    )(tile_expert, n_used, hs, w2)


COMB_TM = BLK


def _moe_combine_kernel(s1_ref, s2_ref, x_ref, route_ref, g_ref, ys_ref, yp_ref, ysm_ref, ya_ref, yb_ref, sem):
    i = pl.program_id(0)
    tm = COMB_TM
    slot = i % 2

    def fetch(tile, sl):
        def issue(r, carry):
            t = tile * tm + r
            _row_copy(ys_ref, s1_ref[t], ya_ref.at[sl], r, sem.at[sl]).start()
            _row_copy(ys_ref, s2_ref[t], yb_ref.at[sl], r, sem.at[sl]).start()
            return carry

        lax.fori_loop(0, tm, issue, 0, unroll=8)

    @pl.when(i == 0)
    def _():
        fetch(0, 0)

    @pl.when(i + 1 < pl.num_programs(0))
    def _():
        fetch(i + 1, 1 - slot)

    def drain(r, carry):
        _row_copy(ys_ref, 0, ya_ref.at[slot], 0, sem.at[slot]).wait()
        _row_copy(ys_ref, 0, yb_ref.at[slot], 0, sem.at[slot]).wait()
        return carry

    lax.fori_loop(0, tm, drain, 0, unroll=8)
    w1 = route_ref[:, 2:3]
    w2 = route_ref[:, 3:4]
    x2 = x_ref[...] + (w1 * ya_ref[slot] + w2 * yb_ref[slot])
    y = _rms(x2, g_ref[...])
    is_prompt = i < B * NB

    @pl.when(is_prompt & (i % NB > 0))
    def _():
        yp_ref[...] = y

    @pl.when(jnp.logical_not(is_prompt))
    def _():
        ysm_ref[...] = y


def moe_combine(x1, route, g, ys, slot1, slot2):
    tm = COMB_TM
    return pl.pallas_call(
        _moe_combine_kernel,
        grid_spec=pltpu.PrefetchScalarGridSpec(
            num_scalar_prefetch=2,
            grid=(M // tm,),
            in_specs=[pl.BlockSpec((tm, D), lambda i, *_: (i, 0)), pl.BlockSpec((tm, ROUTE_W), lambda i, *_: (i, 0)),
                      pl.BlockSpec((1, D), lambda i, *_: (0, 0)), pl.BlockSpec(memory_space=pl.ANY)],
            out_specs=[pl.BlockSpec((tm, D), lambda i, *_: (_prompt_block(i), 0)),
                       pl.BlockSpec((tm, D), lambda i, *_: (jnp.maximum(i - B * NB, 0), 0))],
            scratch_shapes=[pltpu.VMEM((2, tm, D), F32), pltpu.VMEM((2, tm, D), F32),
                            pltpu.SemaphoreType.DMA((2,))]),
        out_shape=[jax.ShapeDtypeStruct((B * SEQ, D), F32), jax.ShapeDtypeStruct((MS, D), F32)],
        compiler_params=_cp("arbitrary"),
        name="moe_combine",
    )(slot1, slot2, x1, route, g.reshape(1, D), ys)


def _moe_plan(route):
    e1 = route[:, 0].astype(jnp.int32)
    e2 = route[:, 1].astype(jnp.int32)
    ids = jnp.arange(NE, dtype=jnp.int32)
    r = jnp.arange(M, dtype=jnp.int32)
    skip = (r < MP) & (r % PB < BLK)
    spare = (r // PB) * BLK + r % PB
    sel = (e1[:, None] == ids).astype(jnp.int32) + (e2[:, None] == ids).astype(jnp.int32)
    sel = jnp.where(skip[:, None], 0, sel)
    incl = jnp.cumsum(sel, axis=0)
    rank = incl - sel
    cnt = incl[-1]
    ntile = (cnt + MOE_TM - 1) // MOE_TM
    tile_end = jnp.cumsum(ntile)
    tile_off = tile_end - ntile
    row_off = tile_off * MOE_TM
    slot1 = jnp.take(row_off, e1) + jnp.take_along_axis(rank, e1[:, None], axis=1)[:, 0]
    slot2 = jnp.take(row_off, e2) + jnp.take_along_axis(rank, e2[:, None], axis=1)[:, 0]
    tiles = jnp.arange(MOE_TILES, dtype=jnp.int32)
    tile_expert = jnp.minimum(jnp.sum((tiles[:, None] >= tile_end[None, :]).astype(jnp.int32), axis=1), NE - 1)
    n_used = tile_end[-1:].astype(jnp.int32)
    tail = jnp.minimum(tile_end[-1] + ids, MOE_TILES - 1)
    zero_tiles = jnp.concatenate([jnp.maximum(tile_end - 1, 0), tail]).astype(jnp.int32)
    scatter = (jnp.where(skip, MOE_ROWS + spare, slot1).astype(jnp.int32),
               jnp.where(skip, MOE_ROWS + MOE_SKIP + spare, slot2).astype(jnp.int32))
    gather = (jnp.where(skip, 0, slot1).astype(jnp.int32), jnp.where(skip, 0, slot2).astype(jnp.int32))
    return scatter, gather, tile_expert.astype(jnp.int32), n_used, zero_tiles


def _blockdiag(w):
    per = GATE_CH // (LW // 16)
    w4 = w.reshape(LW // GATE_CH, per, 64, 64)
    eye = jnp.eye(per, dtype=w.dtype)
    return jnp.einsum("cnde,nm->cndme", w4, eye).reshape(LW // GATE_CH, GATE_CH, GATE_CH).astype(BF16)


def kernel(x_prompt, x_sample, cache_attn_k, cache_attn_v, state_conv, state_lru_conv, state_lru_h, meta_tokens,
           norm_mix, norm_ffn, norm_final, w_in, attn_sinks, w_attn_out, conv_dw_w, conv_dw_b, conv_ln_g, conv_ln_b,
           w_conv_out, lru_conv_w, lru_conv_b, lru_w_r, lru_b_r, lru_w_i, lru_b_i, lru_lambda, w_lru_out, w_mix_out,
           ffn_w1, ffn_w3, ffn_w2, moe_router_w, moe_router_b, moe_w1, moe_w3, moe_w2):
    states_in = (cache_attn_k.reshape(DEPTH * DB, WIN, KVW), cache_attn_v.reshape(DEPTH * DB, WIN, KVW),
                 state_conv.reshape(DEPTH * DB, CK - 1, CW), state_lru_conv.reshape(DEPTH * DB, LCK - 1, LW),
                 state_lru_h.reshape(DEPTH * DB, LW))
    states_out = tuple(jnp.zeros(s.shape, F32) for s in states_in)
    p_k, p_v, p_conv, p_lconv, p_h = [], [], [], [], []

    assert (DEPTH, ffn_w1.shape[0], moe_w1.shape[0]) == (2, 1, 1), \
        "layer 0: dense SwiGLU; layer 1: routed experts followed by the final norm"
    wa, wc, wl, wm = (w.astype(BF16) for w in (w_attn_out, w_conv_out, w_lru_out, w_mix_out))
    w_in0 = w_in[0].astype(BF16)
    ffn_w2_b = ffn_w2[0].astype(BF16)

    def branches(l, z):
        nonlocal states_out
        sinks = attn_sinks[l]
        conv_p = (conv_dw_w[l], conv_dw_b[l], conv_ln_g[l], conv_ln_b[l])
        lru_p = (lru_conv_w[l], lru_conv_b[l], _blockdiag(lru_w_r[l]), lru_b_r[l], _blockdiag(lru_w_i[l]),
                 lru_b_i[l], lru_lambda[l])
        o_p, c_p, rr_p, pc, plc, ph = prompt_branches(z, sinks, conv_p, lru_p)
        o_s, c_s, rr_s, states_out = sample_branches(z, sinks, conv_p, lru_p, states_in, states_out, l)
        kv_last = jnp.stack([lax.slice(z, ((b + 1) * PB - WIN, Z_K), ((b + 1) * PB, Z_K + 2 * KVW))
                             for b in range(B)]).astype(F32)
        p_k.append(kv_last[:, :, :KVW].reshape(B, WIN, NKV, HD))
        p_v.append(kv_last[:, :, KVW:].reshape(B, WIN, NKV, HD))
        p_conv.append(pc)
        p_lconv.append(plc)
        p_h.append(ph.reshape(B, LW))
        return (o_p, o_s), (c_p, c_s), (rr_p, rr_s)

    x, xn = embed_norm(x_prompt, x_sample, meta_tokens, norm_mix[0])

    z, (moe_w1_b, ffn_w1_b, ffn_w3_b) = inproj(
        xn, w_in0, [(moe_w1.reshape(-1, DFF), 128, None), (ffn_w1, 128, 0), (ffn_w3, 128, 0)])
    o, c, rr = branches(0, z)
    x1, hn = mix(o, c, rr, z, x, wa, wc, wl, wm, norm_ffn[0], 0)
    h, (moe_w2_b,) = ffn_up(hn, ffn_w1_b, ffn_w3_b, [(moe_w2.reshape(-1, D), 512, None)])
    x, xn, (w_in1,) = ffn_down(h, ffn_w2_b, x1, norm_mix[1], [(w_in, 64, 1)])

    z, (moe_w3_b,) = inproj(xn, w_in1, [(moe_w3.reshape(-1, DFF), 128, None)])
    o, c, rr = branches(1, z)
    rw = jnp.pad(moe_router_w[0], ((0, 0), (0, ROUTE_W - NE)))
    rw_hi = rw.astype(BF16)
    rw_lo = (rw - rw_hi.astype(F32)).astype(BF16)
    rb = jnp.pad(moe_router_b[0].reshape(1, NE), ((0, 0), (0, ROUTE_W - NE)))
    x1, route = mix(o, c, rr, z, x, wa, wc, wl, wm, norm_ffn[1], 1,
                    router=(jnp.concatenate([rw_hi, rw_lo], axis=1), rb))
    scatter_slots, gather_slots, tile_expert, n_used, zero_tiles = _moe_plan(route)
    xs = moe_scatter(x1, norm_ffn[1], *scatter_slots, zero_tiles)
    hs = moe_up(xs, moe_w1_b.reshape(moe_w1.shape), moe_w3_b.reshape(moe_w3.shape), tile_expert, n_used, 0)
    ys = moe_down(hs, moe_w2_b.reshape(moe_w2.shape), tile_expert, n_used, 0)
    y_prompt, y_sample = moe_combine(x1, route, norm_final, ys, *gather_slots)

    y_prompt = y_prompt.reshape(B, SEQ, D)
    y_sample = y_sample.reshape(DB, T, D)
    s_k, s_v, s_conv, s_lconv, s_h = states_out
    st4 = lambda a: a.reshape(DEPTH, DB, WIN, NKV, HD)
    return (y_prompt, y_sample,
            jnp.stack(p_k), jnp.stack(p_v), jnp.stack(p_conv), jnp.stack(p_lconv), jnp.stack(p_h),
            st4(s_k), st4(s_v), s_conv.reshape(DEPTH, DB, CK - 1, CW), s_lconv.reshape(DEPTH, DB, LCK - 1, LW),
            s_h.reshape(DEPTH, DB, LW))
```

```python
import functools
import math

import jax
import jax.numpy as jnp
from jax import lax
from jax.experimental import pallas as pl
from jax.experimental.pallas import tpu as pltpu

F32 = jnp.float32
BF16 = jnp.bfloat16

D = 2048
B = 2
SEQ = 4096
DEPTH = 2
DB = 128
T = 8
PAST = 8192
N_META = 16
NH = 16
NKV = 4
G = NH // NKV
HD = 64
AW = NH * HD
KVW = NKV * HD
WIN = 128
BLK = 128
CW = D // 2
CK = 31
LW = D // 2
LCK = 4
LRU_C = 8.0
DFF = 5632
NE = 8
EPS = 1e-6
NEG = -1e30
IN_W = AW + 2 * KVW + 2 * CW + 2 * LW + 3 * D

PAD = (-N_META) % BLK
PB = PAD + N_META + SEQ
NB = PB // BLK
MP = B * PB
MS = DB * T
M = MP + MS

ZC = 512
NZC = IN_W // ZC
Z_Q, Z_A, Z_GATE, Z_LX, Z_LG, Z_GA, Z_K, Z_V = 0, 1024, 2048, 3072, 4096, 5120, 11264, 11520

SLOPES = tuple(2.0 ** (-8.0 * (h + 1.0) / NH) for h in range(NH))

TM_BIG = 1184
TM_MID = 592
TM_TOK = 256
SB = 8
MOE_TM = 256
MOE_SKIP = B * BLK
MOE_TILES = (2 * (M - MOE_SKIP)) // MOE_TM + NE
MOE_ROWS = MOE_TILES * MOE_TM
UP_TN = 1408
DOWN_TN = 1024

VMEM_LIMIT = 56 * 1024 * 1024


def _cp(*sem):
    return pltpu.CompilerParams(dimension_semantics=sem, vmem_limit_bytes=VMEM_LIMIT)


def _rms(x, g):
    return x * lax.rsqrt(jnp.mean(x * x, axis=-1, keepdims=True) + EPS) * g


def _sigmoid(x):
    return 1.0 / (1.0 + jnp.exp(-x))


def _silu(x):
    return x * _sigmoid(x)


def _prompt_block(i):
    ic = jnp.minimum(i, B * NB - 1)
    return (ic // NB) * (NB - 1) + jnp.maximum(ic % NB - 1, 0)


def _embed_norm_kernel(xp_ref, xs_ref, meta_ref, g_ref, x_ref, xn_ref):
    i = pl.program_id(0)
    head = jnp.concatenate([jnp.zeros((PAD, D), F32), meta_ref[...]], axis=0)
    x = jnp.where(i % NB == 0, head, xp_ref[...])
    x = jnp.where(i < B * NB, x, xs_ref[...])
    x_ref[...] = x
    xn_ref[...] = _rms(x, g_ref[...]).astype(xn_ref.dtype)


def embed_norm(x_prompt, x_sample, meta, g):
    tm = BLK
    return pl.pallas_call(
        _embed_norm_kernel,
        grid=(M // tm,),
        in_specs=[pl.BlockSpec((tm, D), lambda i: (_prompt_block(i), 0)),
                  pl.BlockSpec((tm, D), lambda i: (jnp.maximum(i - B * NB, 0), 0)),
                  pl.BlockSpec((N_META, D), lambda i: (0, 0)), pl.BlockSpec((1, D), lambda i: (0, 0))],
        out_specs=[pl.BlockSpec((tm, D), lambda i: (i, 0)), pl.BlockSpec((tm, D), lambda i: (i, 0))],
        out_shape=[jax.ShapeDtypeStruct((M, D), F32), jax.ShapeDtypeStruct((M, D), BF16)],
        compiler_params=_cp("parallel"),
        name="embed_norm",
    )(x_prompt.reshape(B * SEQ, D), x_sample.reshape(MS, D), meta, g.reshape(1, D))


def _side_plan(streams, step_of, total_steps):
    plan, in_specs, out_specs, out_shapes = [], [], [], []
    first = 0
    for arr, rows, lead in streams:
        nrows, ncols = arr.shape[-2:]
        nblk = nrows // rows
        assert nblk * rows == nrows

        def block(*g, first=first, nblk=nblk):
            return jnp.clip(step_of(*g) - first, 0, nblk - 1)

        if lead is None:
            in_specs.append(pl.BlockSpec((rows, ncols), lambda *g, block=block: (block(*g), 0)))
        else:
            in_specs.append(pl.BlockSpec((None, rows, ncols), lambda *g, block=block, lead=lead: (lead, block(*g), 0)))
        out_specs.append(pl.BlockSpec((rows, ncols), lambda *g, block=block: (block(*g), 0)))
        out_shapes.append(jax.ShapeDtypeStruct((nrows, ncols), BF16))
        plan.append((first, nblk))
        first += nblk
    assert first <= total_steps
    return plan, in_specs, out_specs, out_shapes


def _run_side(plan, src_refs, dst_refs, step):
    for (first, nblk), src_ref, dst_ref in zip(plan, src_refs, dst_refs):
        @pl.when((step >= first) & (step < first + nblk))
        def _():
            dst_ref[...] = src_ref[...].astype(dst_ref.dtype)


def _inproj_kernel(x_ref, w_ref, *refs, plan):
    n = len(plan)
    o_ref = refs[n]
    o_ref[...] = jnp.dot(x_ref[...], w_ref[...], preferred_element_type=F32).astype(o_ref.dtype)
    _run_side(plan, refs[:n], refs[n + 1:], pl.program_id(0) * NZC + pl.program_id(1))


def _z_src_block(j):
    return jnp.where(j < 2, j, jnp.where(j < NZC - 1, j + 1, 2))


def inproj(xn, w, streams):
    tm = TM_BIG
    steps = (M // tm) * NZC
    plan, s_in, s_out, s_shapes = _side_plan(streams, lambda i, j: i * NZC + j, steps)
    outs = pl.pallas_call(
        functools.partial(_inproj_kernel, plan=plan),
        grid=(M // tm, NZC),
        in_specs=[pl.BlockSpec((tm, D), lambda i, j: (i, 0)),
                  pl.BlockSpec((D, ZC), lambda i, j: (0, _z_src_block(j)))] + s_in,
        out_specs=[pl.BlockSpec((tm, ZC), lambda i, j: (i, j))] + s_out,
        out_shape=[jax.ShapeDtypeStruct((M, IN_W), BF16)] + s_shapes,
        compiler_params=_cp("arbitrary", "arbitrary"),
        name="inproj",
    )(xn, w, *[s[0] for s in streams])
    return outs[0], outs[1:]


def _softmax_sink(s, sink):
    m = jnp.maximum(jnp.max(s, axis=-1, keepdims=True), sink)
    e = jnp.exp(s - m)
    l = jnp.sum(e, axis=-1, keepdims=True) + jnp.exp(sink - m)
    return e / l


def _attn_prompt_start(n, bias_ref):
    @pl.when(n == 0)
    def _():
        r = lax.broadcasted_iota(jnp.int32, (BLK, 2 * BLK), 0)
        c = lax.broadcasted_iota(jnp.int32, (BLK, 2 * BLK), 1)
        dist = BLK + r - c
        in_window = (dist >= 0) & (dist <= WIN)
        distf = dist.astype(F32)
        for hq in range(NH):
            bias_ref[hq] = jnp.where(in_window, -SLOPES[hq] * distf, NEG)


def _attn_prompt_main(n, sink_ref, q_ref, kp_ref, kc_ref, vp_ref, vc_ref, o_ref, bias_ref):
    c = lax.broadcasted_iota(jnp.int32, (BLK, 2 * BLK), 1)
    exists = (n - 1) * BLK - PAD + c >= 0
    q = q_ref[...] * (HD ** -0.5)
    k = jnp.concatenate([kp_ref[...], kc_ref[...]], axis=0)
    v = jnp.concatenate([vp_ref[...], vc_ref[...]], axis=0)
    for h in range(NKV):
        kh = k[:, h * HD:(h + 1) * HD]
        vh = v[:, h * HD:(h + 1) * HD]
        qh = jnp.concatenate([q[:, (h * G + g) * HD:(h * G + g + 1) * HD] for g in range(G)], axis=0)
        s = lax.dot_general(qh, kh, (((1,), (1,)), ((), ())), preferred_element_type=F32)
        ps = []
        for g in range(G):
            hq = h * G + g
            sg = jnp.where(exists, s[g * BLK:(g + 1) * BLK] + bias_ref[hq], NEG)
            ps.append(_softmax_sink(sg, sink_ref[hq]).astype(BF16))
        oh = jnp.dot(jnp.concatenate(ps, axis=0), vh, preferred_element_type=F32)
        for g in range(G):
            hq = h * G + g
            o_ref[:, hq * HD:(hq + 1) * HD] = oh[g * BLK:(g + 1) * BLK].astype(o_ref.dtype)


def _attn_sample_body(sink_ref, q_ref, kn_ref, vn_ref, ck_ref, cv_ref, o_ref, nk_ref, nv_ref):
    q = q_ref[...].astype(F32).reshape(SB, T, AW)
    kn = kn_ref[...].astype(F32).reshape(SB, T, KVW)
    vn = vn_ref[...].astype(F32).reshape(SB, T, KVW)
    ck = ck_ref[...]
    cv = cv_ref[...]
    wb = WIN
    nk_ref[:, 0:wb - T, :] = ck[:, T:wb, :]
    nk_ref[:, wb - T:wb, :] = kn
    nv_ref[:, 0:wb - T, :] = cv[:, T:wb, :]
    nv_ref[:, wb - T:wb, :] = vn
    zpad = jnp.zeros((SB, wb - T, KVW), F32)
    k = jnp.concatenate([ck, kn, zpad], axis=1).astype(BF16)
    v = jnp.concatenate([cv, vn, zpad], axis=1).astype(BF16)
    nkeys = 2 * wb
    r = lax.broadcasted_iota(jnp.int32, (G * T, nkeys), 0)
    c = lax.broadcasted_iota(jnp.int32, (G * T, nkeys), 1)
    dist = wb + (r % T) - c
    allowed = (dist >= 0) & (dist <= WIN)
    distf = dist.astype(F32)
    gidx = r // T
    for h in range(NKV):
        slope = jnp.zeros((G * T, nkeys), F32)
        sink = jnp.zeros((G * T, 1), F32)
        for g in range(G):
            slope = jnp.where(gidx == g, SLOPES[h * G + g], slope)
            sink = jnp.where(gidx[:, 0:1] == g, sink_ref[h * G + g], sink)
        kh = k[:, :, h * HD:(h + 1) * HD]
        vh = v[:, :, h * HD:(h + 1) * HD]
        qh = jnp.concatenate([q[:, :, (h * G + g) * HD:(h * G + g + 1) * HD] for g in range(G)],
                             axis=1).astype(BF16)
        s = jnp.einsum("bqd,bkd->bqk", qh, kh, preferred_element_type=F32)
        s = s * (HD ** -0.5) - (slope * distf)[None]
        s = jnp.where(allowed[None], s, NEG)
        p = _softmax_sink(s, sink[None]).astype(BF16)
        oh = jnp.einsum("bqk,bkd->bqd", p, vh, preferred_element_type=F32)
        for g in range(G):
            hq = h * G + g
            piece = oh[:, g * T:(g + 1) * T, :].reshape(SB * T, HD)
            o_ref[:, hq * HD:(hq + 1) * HD] = piece.astype(o_ref.dtype)


def _ln_swish(y, g, b):
    mu = jnp.mean(y, axis=-1, keepdims=True)
    yc = y - mu
    var = jnp.mean(yc * yc, axis=-1, keepdims=True)
    yn = yc * lax.rsqrt(var + EPS) * g + b
    return _silu(yn)


CONV_HIST = 32
SUBLANES = 8


def _dwconv_by_phase(load, wrow, nrows, first, ntaps):
    y = None
    for s in range(SUBLANES):
        n = nrows if s == 0 else nrows + SUBLANES
        part = None
        for q in range((first + ntaps - 1) // SUBLANES + 1):
            o = SUBLANES * q + s
            if first <= o < first + ntaps:
                term = wrow(o - first) * load(SUBLANES * q, n)
                part = term if part is None else part + term
        if part is not None:
            part = part[s:s + nrows]
            y = part if y is None else y + part
    return y


def _conv_prompt_start(n, ext_ref):
    @pl.when(n == 0)
    def _():
        ext_ref[0:CONV_HIST, :] = jnp.zeros((CONV_HIST, CW), F32)

    @pl.when(n > 0)
    def _():
        ext_ref[0:CONV_HIST, :] = ext_ref[BLK:BLK + CONV_HIST, :]


def _conv_prompt_main(n, a_ref, g_ref, w_ref, b_ref, lg_ref, lb_ref, c_ref, ext_ref, y_ref):
    u = a_ref[...].astype(F32) * _sigmoid(g_ref[...].astype(F32))
    row = lax.broadcasted_iota(jnp.int32, (BLK, 1), 0)
    u = jnp.where((n > 0) | (row >= PAD), u, 0.0)
    ext_ref[CONV_HIST:CONV_HIST + BLK, :] = u
    off = CONV_HIST - (CK - 1)
    cc = 256
    for c0 in range(0, CW, cc):
        cs = slice(c0, c0 + cc)
        y_ref[:, cs] = b_ref[:, cs] + _dwconv_by_phase(lambda r0, nr: ext_ref[r0:r0 + nr, cs],
                                                      lambda j: w_ref[j:j + 1, cs], BLK, off, CK)
    c_ref[...] = _ln_swish(y_ref[...], lg_ref[...], lb_ref[...]).astype(c_ref.dtype)


def _conv_prompt_finish(n, st_ref, ext_ref):
    @pl.when(n == NB - 1)
    def _():
        st_ref[0] = ext_ref[CONV_HIST + BLK - (CK - 1):CONV_HIST + BLK, :]


def _conv_sample_body(a_ref, g_ref, st_ref, w_ref, b_ref, lg_ref, lb_ref, c_ref, nst_ref, ext_ref, y_ref):
    u = a_ref[...].astype(F32) * _sigmoid(g_ref[...].astype(F32))
    hist = CK - 1
    base = CONV_HIST - hist
    for s in range(SB):
        ext_ref[s, 0:base, :] = jnp.zeros((base, CW), F32)
        ext_ref[s, base:CONV_HIST, :] = st_ref[s]
        ext_ref[s, CONV_HIST:CONV_HIST + T, :] = u[s * T:(s + 1) * T]
    for s in range(SB):
        y_ref[s * T:(s + 1) * T, :] = b_ref[...] + _dwconv_by_phase(
            lambda r0, nr: ext_ref[s, r0:r0 + nr, :], lambda j: w_ref[j:j + 1, :], T, base, CK)
        nst_ref[s] = ext_ref[s, base + T:base + T + hist, :]
    c_ref[...] = _ln_swish(y_ref[...], lg_ref[...], lb_ref[...]).astype(c_ref.dtype)


LRU_HIST = 8
GATE_CH = 256


def _gelu_tanh(x):
    return x * (0.5 * (1.0 + jnp.tanh(math.sqrt(2.0 / math.pi) * (x + 0.044715 * (x * x * x)))))


def _softplus(x):
    return jnp.maximum(x, 0.0) + jnp.log1p(jnp.exp(-jnp.abs(x)))


def _expm1(x):
    return jnp.tanh(0.5 * x) * (jnp.exp(x) + 1.0)


def _lru_gates(xc, wr_ref, br_ref, wi_ref, bi_ref, lam_ref, valid, a_ref, bx_ref):
    sp = _softplus(-lam_ref[...])
    for k in range(LW // GATE_CH):
        sl = slice(k * GATE_CH, (k + 1) * GATE_CH)
        xk = xc[:, sl]
        xkb = xk.astype(BF16)
        r = _sigmoid(jnp.dot(xkb, wr_ref[k], preferred_element_type=F32) + br_ref[:, sl])
        i = _sigmoid(jnp.dot(xkb, wi_ref[k], preferred_element_type=F32) + bi_ref[:, sl])
        log_a = (-LRU_C) * r * sp[:, sl]
        a = jnp.exp(log_a)
        bx = jnp.sqrt(-_expm1(2.0 * log_a)) * (i * xk)
        if valid is not None:
            bx = jnp.where(valid, bx, 0.0)
        a_ref[:, sl] = a
        bx_ref[:, sl] = bx


def _lru_prompt_start(n, ext_ref, carry_ref):
    @pl.when(n == 0)
    def _():
        ext_ref[0:LRU_HIST, :] = jnp.zeros((LRU_HIST, LW), F32)
        carry_ref[...] = jnp.zeros((1, LW), F32)

    @pl.when(n > 0)
    def _():
        ext_ref[0:LRU_HIST, :] = ext_ref[BLK:BLK + LRU_HIST, :]


def _lru_prompt_main(n, x_ref, g_ref, cw_ref, cb_ref, wr_ref, br_ref, wi_ref, bi_ref, lam_ref,
                     rr_ref, ext_ref, a_ref, bx_ref, h_ref, carry_ref):
    row = lax.broadcasted_iota(jnp.int32, (BLK, 1), 0)
    valid = (n > 0) | (row >= PAD)
    ext_ref[LRU_HIST:LRU_HIST + BLK, :] = jnp.where(valid, x_ref[...].astype(F32), 0.0)
    off = LRU_HIST - (LCK - 1)
    xc = jnp.zeros((BLK, LW), F32) + cb_ref[...]
    for j in range(LCK):
        xc = xc + cw_ref[j:j + 1, :] * ext_ref[off + j:off + j + BLK, :]
    _lru_gates(xc, wr_ref, br_ref, wi_ref, bi_ref, lam_ref, valid, a_ref, bx_ref)

    h = carry_ref[...]
    for t in range(BLK):
        h = a_ref[t:t + 1, :] * h + bx_ref[t:t + 1, :]
        h_ref[t:t + 1, :] = h
    carry_ref[...] = h
    rr_ref[...] = (h_ref[...] * _gelu_tanh(g_ref[...].astype(F32))).astype(rr_ref.dtype)


def _lru_prompt_finish(n, cst_ref, hst_ref, ext_ref, carry_ref):
    @pl.when(n == NB - 1)
    def _():
        cst_ref[0] = ext_ref[LRU_HIST + BLK - (LCK - 1):LRU_HIST + BLK, :]
        hst_ref[0] = carry_ref[...]


def _prompt_branches_kernel(sink_ref, q_ref, kp_ref, kc_ref, vp_ref, vc_ref, ca_ref, cg_ref, lx_ref, lg_ref,
                            dw_ref, db_ref, lng_ref, lnb_ref,
                            cw_ref, cb_ref, wr_ref, br_ref, wi_ref, bi_ref, lam_ref,
                            o_ref, c_ref, rr_ref, cst_ref, lcst_ref, hst_ref,
                            bias_ref, cext_ref, cy_ref, lext_ref, a_ref, bx_ref, h_ref, carry_ref):
    n = pl.program_id(1)
    _attn_prompt_start(n, bias_ref)
    _conv_prompt_start(n, cext_ref)
    _lru_prompt_start(n, lext_ref, carry_ref)
    _attn_prompt_main(n, sink_ref, q_ref, kp_ref, kc_ref, vp_ref, vc_ref, o_ref, bias_ref)
    _conv_prompt_main(n, ca_ref, cg_ref, dw_ref, db_ref, lng_ref, lnb_ref, c_ref, cext_ref, cy_ref)
    _lru_prompt_main(n, lx_ref, lg_ref, cw_ref, cb_ref, wr_ref, br_ref, wi_ref, bi_ref, lam_ref,
                     rr_ref, lext_ref, a_ref, bx_ref, h_ref, carry_ref)
    _conv_prompt_finish(n, cst_ref, cext_ref)
    _lru_prompt_finish(n, lcst_ref, hst_ref, lext_ref, carry_ref)


def prompt_branches(z, sinks, conv_p, lru_p):
    kb, vb = Z_K // KVW, Z_V // KVW
    cur = lambda col: (lambda b, n: (b * NB + n, col))
    prev = lambda col: (lambda b, n: (b * NB + jnp.maximum(n - 1, 0), col))
    full = lambda shape: pl.BlockSpec(shape, lambda b, n: (0,) * len(shape))
    wide = lambda col: pl.BlockSpec((BLK, 1024), cur(col))
    vec = full((1, 1024))
    wspec = full((LW // GATE_CH, GATE_CH, GATE_CH))
    dw_w, dw_b, ln_g, ln_b = conv_p
    cw, cb, wr, br, wi, bi, lam = lru_p
    row = lambda v: v.reshape(1, -1)
    blk = pltpu.VMEM((BLK, 1024), F32)
    return pl.pallas_call(
        _prompt_branches_kernel,
        grid=(B, NB),
        in_specs=[pl.BlockSpec(memory_space=pltpu.SMEM),
                  wide(0),
                  pl.BlockSpec((BLK, KVW), prev(kb)), pl.BlockSpec((BLK, KVW), cur(kb)),
                  pl.BlockSpec((BLK, KVW), prev(vb)), pl.BlockSpec((BLK, KVW), cur(vb)),
                  wide(Z_A // 1024), wide(Z_GATE // 1024), wide(Z_LX // 1024), wide(Z_LG // 1024),
                  full((CK, CW)), vec, vec, vec,
                  full((LCK, LW)), vec, wspec, vec, wspec, vec, vec],
        out_specs=[wide(0), wide(0), wide(0),
                   pl.BlockSpec((1, CK - 1, CW), lambda b, n: (b, 0, 0)),
                   pl.BlockSpec((1, LCK - 1, LW), lambda b, n: (b, 0, 0)),
                   pl.BlockSpec((1, 1, LW), lambda b, n: (b, 0, 0))],
        out_shape=[jax.ShapeDtypeStruct((MP, AW), BF16), jax.ShapeDtypeStruct((MP, CW), BF16),
                   jax.ShapeDtypeStruct((MP, LW), BF16), jax.ShapeDtypeStruct((B, CK - 1, CW), F32),
                   jax.ShapeDtypeStruct((B, LCK - 1, LW), F32), jax.ShapeDtypeStruct((B, 1, LW), F32)],
        scratch_shapes=[pltpu.VMEM((NH, BLK, 2 * BLK), F32),
                        pltpu.VMEM((CONV_HIST + BLK, CW), F32), blk,
                        pltpu.VMEM((LRU_HIST + BLK, LW), F32), blk, blk, blk, pltpu.VMEM((1, LW), F32)],
        compiler_params=_cp("parallel", "arbitrary"),
        name="prompt_branches",
    )(sinks, z, z, z, z, z, z, z, z, z, dw_w, row(dw_b), row(ln_g), row(ln_b),
      cw, row(cb), wr, row(br), wi, row(bi), row(lam))


def _lru_sample_body(x_ref, g_ref, cst_ref, h0_ref, cw_ref, cb_ref, wr_ref, br_ref, wi_ref, bi_ref, lam_ref,
                     rr_ref, ncst_ref, nh_ref, ext_ref, xc_ref, a_ref, bx_ref, h_ref):
    hist = LCK - 1
    base = LRU_HIST - hist
    x = x_ref[...].astype(F32)
    for s in range(SB):
        ext_ref[s, base:LRU_HIST, :] = cst_ref[s]
        ext_ref[s, LRU_HIST:LRU_HIST + T, :] = x[s * T:(s + 1) * T]
    for s in range(SB):
        acc = jnp.zeros((T, LW), F32) + cb_ref[...]
        for j in range(LCK):
            acc = acc + cw_ref[j:j + 1, :] * ext_ref[s, base + j:base + j + T, :]
        xc_ref[s * T:(s + 1) * T, :] = acc
        ncst_ref[s] = ext_ref[s, base + T:base + T + hist, :]
    _lru_gates(xc_ref[...], wr_ref, br_ref, wi_ref, bi_ref, lam_ref, None, a_ref, bx_ref)
    for s in range(SB):
        h = h0_ref[s:s + 1, :]
        for t in range(T):
            rw = s * T + t
            h = a_ref[rw:rw + 1, :] * h + bx_ref[rw:rw + 1, :]
            h_ref[rw:rw + 1, :] = h
        nh_ref[s:s + 1, :] = h
    rr_ref[...] = (h_ref[...] * _gelu_tanh(g_ref[...].astype(F32))).astype(rr_ref.dtype)


def _sample_branches_kernel(sink_ref, q_ref, kn_ref, vn_ref, ck_ref, cv_ref, ca_ref, cg_ref, cst_ref,
                            dw_ref, db_ref, lng_ref, lnb_ref, lx_ref, lg_ref, lcst_ref, h0_ref,
                            cw_ref, cb_ref, wr_ref, br_ref, wi_ref, bi_ref, lam_ref,
                            nk_in, nv_in, nst_in, ncst_in, nh_in,
                            o_ref, nk_ref, nv_ref, c_ref, nst_ref, rr_ref, ncst_ref, nh_ref,
                            cext_ref, cy_ref, lext_ref, xc_ref, a_ref, bx_ref, h_ref):
    del nk_in, nv_in, nst_in, ncst_in, nh_in
    _attn_sample_body(sink_ref, q_ref, kn_ref, vn_ref, ck_ref, cv_ref, o_ref, nk_ref, nv_ref)
    _conv_sample_body(ca_ref, cg_ref, cst_ref, dw_ref, db_ref, lng_ref, lnb_ref, c_ref, nst_ref, cext_ref, cy_ref)
    _lru_sample_body(lx_ref, lg_ref, lcst_ref, h0_ref, cw_ref, cb_ref, wr_ref, br_ref, wi_ref, bi_ref, lam_ref,
                     rr_ref, ncst_ref, nh_ref, lext_ref, xc_ref, a_ref, bx_ref, h_ref)


def sample_branches(z, sinks, conv_p, lru_p, states_in, states_out, layer):
    rows = SB * T
    rb0 = MP // rows
    sb0 = layer * (DB // SB)
    kb, vb = Z_K // KVW, Z_V // KVW
    zrow = lambda w, col: pl.BlockSpec((rows, w), lambda i: (rb0 + i, col))
    full = lambda shape: pl.BlockSpec(shape, lambda i: (0,) * len(shape))
    st3 = lambda n, w: pl.BlockSpec((SB, n, w), lambda i: (sb0 + i, 0, 0))
    vec = full((1, 1024))
    wspec = full((LW // GATE_CH, GATE_CH, GATE_CH))
    kvspec, cspec, lcspec = st3(WIN, KVW), st3(CK - 1, CW), st3(LCK - 1, LW)
    hspec = pl.BlockSpec((SB, LW), lambda i: (sb0 + i, 0))
    anyspec = pl.BlockSpec(memory_space=pl.ANY)
    out_row = pl.BlockSpec((rows, 1024), lambda i: (i, 0))
    dw_w, dw_b, ln_g, ln_b = conv_p
    cw, cb, wr, br, wi, bi, lam = lru_p
    row = lambda v: v.reshape(1, -1)
    blk = pltpu.VMEM((rows, 1024), F32)
    shape_of = lambda a: jax.ShapeDtypeStruct(a.shape, a.dtype)
    n_in = 24
    outs = pl.pallas_call(
        _sample_branches_kernel,
        grid=(DB // SB,),
        in_specs=[pl.BlockSpec(memory_space=pltpu.SMEM),
                  zrow(AW, 0), zrow(KVW, kb), zrow(KVW, vb), kvspec, kvspec,
                  zrow(CW, Z_A // CW), zrow(CW, Z_GATE // CW), cspec, full((CK, CW)), vec, vec, vec,
                  zrow(LW, Z_LX // LW), zrow(LW, Z_LG // LW), lcspec, hspec,
                  full((LCK, LW)), vec, wspec, vec, wspec, vec, vec] + [anyspec] * 5,
        out_specs=[out_row, kvspec, kvspec, out_row, cspec, out_row, lcspec, hspec],
        out_shape=[jax.ShapeDtypeStruct((MS, AW), BF16), shape_of(states_out[0]), shape_of(states_out[1]),
                   jax.ShapeDtypeStruct((MS, CW), BF16), shape_of(states_out[2]),
                   jax.ShapeDtypeStruct((MS, LW), BF16), shape_of(states_out[3]), shape_of(states_out[4])],
        scratch_shapes=[pltpu.VMEM((SB, CONV_HIST + T, CW), F32), blk,
                        pltpu.VMEM((SB, LRU_HIST + T, LW), F32), blk, blk, blk, blk],
        input_output_aliases={n_in: 1, n_in + 1: 2, n_in + 2: 4, n_in + 3: 6, n_in + 4: 7},
        compiler_params=_cp("parallel"),
        name="sample_branches",
    )(sinks, z, z, z, states_in[0], states_in[1], z, z, states_in[2], dw_w, row(dw_b), row(ln_g), row(ln_b),
      z, z, states_in[3], states_in[4], cw, row(cb), wr, row(br), wi, row(bi), row(lam), *states_out)
    o, nk, nv, c, nst, rr, ncst, nh = outs
    return o, c, rr, (nk, nv, nst, ncst, nh)


ROUTE_W = 128


def _mix_body(branch_refs, gates, x_ref, wa_ref, wc_ref, wl_ref, wm_ref, g_ref, x1_ref):
    is_prompt = pl.program_id(0) < MP // TM_TOK
    o, c, r = (jnp.where(is_prompt, p_ref[...], s_ref[...]) for p_ref, s_ref in branch_refs)
    half = D // 2
    parts = []
    for hh in range(2):
        sl = slice(hh * half, (hh + 1) * half)
        m = _sigmoid(gates[0][hh][...].astype(F32)) * jnp.dot(o, wa_ref[:, sl], preferred_element_type=F32)
        m = m + _sigmoid(gates[1][hh][...].astype(F32)) * jnp.dot(c, wc_ref[:, sl], preferred_element_type=F32)
        m = m + _sigmoid(gates[2][hh][...].astype(F32)) * jnp.dot(r, wl_ref[:, sl], preferred_element_type=F32)
        parts.append(m.astype(BF16))
    merged = jnp.concatenate(parts, axis=1)
    x1 = x_ref[...] + jnp.dot(merged, wm_ref[...], preferred_element_type=F32)
    x1_ref[...] = x1
    return _rms(x1, g_ref[...])


def _mix_kernel(op_ref, os_ref, cp_ref, cs_ref, rp_ref, rs_ref, ga0, ga1, gb0, gb1, gc0, gc1, x_ref,
                wa_ref, wc_ref, wl_ref, wm_ref, g_ref, x1_ref, hn_ref):
    hn = _mix_body(((op_ref, os_ref), (cp_ref, cs_ref), (rp_ref, rs_ref)), ((ga0, ga1), (gb0, gb1), (gc0, gc1)),
                   x_ref, wa_ref, wc_ref, wl_ref, wm_ref, g_ref, x1_ref)
    hn_ref[...] = hn.astype(hn_ref.dtype)


def _mix_router_kernel(op_ref, os_ref, cp_ref, cs_ref, rp_ref, rs_ref, ga0, ga1, gb0, gb1, gc0, gc1, x_ref,
                       wa_ref, wc_ref, wl_ref, wm_ref, g_ref, rw_ref, rb_ref, x1_ref, route_ref):
    hn = _mix_body(((op_ref, os_ref), (cp_ref, cs_ref), (rp_ref, rs_ref)), ((ga0, ga1), (gb0, gb1), (gc0, gc1)),
                   x_ref, wa_ref, wc_ref, wl_ref, wm_ref, g_ref, x1_ref)
    h_hi = hn.astype(BF16)
    h_lo = (hn - h_hi.astype(F32)).astype(BF16)
    both = jnp.dot(h_hi, rw_ref[...], preferred_element_type=F32)
    logits = (both[:, :ROUTE_W] + both[:, ROUTE_W:]
              + jnp.dot(h_lo, rw_ref[:, :ROUTE_W], preferred_element_type=F32) + rb_ref[...])
    lane = lax.broadcasted_iota(jnp.int32, logits.shape, 1)
    ninf = -jnp.inf
    l1 = jnp.where(lane < NE, logits, ninf)
    m1 = jnp.max(l1, axis=-1, keepdims=True)
    i1 = jnp.min(jnp.where(l1 == m1, lane, ROUTE_W), axis=-1, keepdims=True)
    l2 = jnp.where(lane == i1, ninf, l1)
    m2 = jnp.max(l2, axis=-1, keepdims=True)
    i2 = jnp.min(jnp.where(l2 == m2, lane, ROUTE_W), axis=-1, keepdims=True)
    e2 = jnp.exp(m2 - m1)
    den = 1.0 + e2
    out = jnp.where(lane == 0, i1.astype(F32), 0.0)
    out = jnp.where(lane == 1, i2.astype(F32), out)
    out = jnp.where(lane == 2, 1.0 / den, out)
    out = jnp.where(lane == 3, e2 / den, out)
    route_ref[...] = out


def mix(o, c, rr, z, x, wa, wc, wl, wm, g, layer, router=None):
    tm = TM_TOK
    half = D // 2
    npt = MP // tm
    row = lambda col: (lambda i: (i, col))
    const = lambda shape: pl.BlockSpec(shape, lambda i: (0,) * len(shape), pipeline_mode=pl.Buffered(1))
    wspec = lambda k: pl.BlockSpec((None, k, D), lambda i: (layer, 0, 0), pipeline_mode=pl.Buffered(1))
    gate_specs = [pl.BlockSpec((tm, half), row(Z_GA // half + k)) for k in range(6)]
    pair = lambda w: [pl.BlockSpec((tm, w), lambda i: (jnp.minimum(i, npt - 1), 0)),
                      pl.BlockSpec((tm, w), lambda i: (jnp.maximum(i - npt, 0), 0))]
    in_specs = (pair(AW) + pair(CW) + pair(LW) + gate_specs
                + [pl.BlockSpec((tm, D), row(0)), wspec(AW), wspec(CW), wspec(LW), wspec(D), const((1, D))])
    args = [*o, *c, *rr, z, z, z, z, z, z, x, wa, wc, wl, wm, g.reshape(1, D)]
    if router is None:
        kern = _mix_kernel
        out_specs = [pl.BlockSpec((tm, D), row(0)), pl.BlockSpec((tm, D), row(0))]
        out_shape = [jax.ShapeDtypeStruct((M, D), F32), jax.ShapeDtypeStruct((M, D), BF16)]
    else:
        kern = _mix_router_kernel
        rw, rb = router
        in_specs += [const((D, 2 * ROUTE_W)), const((1, ROUTE_W))]
        args += [rw, rb]
        out_specs = [pl.BlockSpec((tm, D), row(0)), pl.BlockSpec((tm, ROUTE_W), row(0))]
        out_shape = [jax.ShapeDtypeStruct((M, D), F32), jax.ShapeDtypeStruct((M, ROUTE_W), F32)]
    return pl.pallas_call(
        kern,
        grid=(M // tm,),
        in_specs=in_specs,
        out_specs=out_specs,
        out_shape=out_shape,
        compiler_params=_cp("parallel"),
        name="mix",
    )(*args)


def _ffn_up_kernel(x_ref, w1_ref, w3_ref, *refs, plan):
    n = len(plan)
    o_ref = refs[n]
    x = x_ref[...]
    a = jnp.dot(x, w1_ref[...], preferred_element_type=F32)
    b = jnp.dot(x, w3_ref[...], preferred_element_type=F32)
    o_ref[...] = (_silu(a) * b).astype(o_ref.dtype)
    _run_side(plan, refs[:n], refs[n + 1:], pl.program_id(0) * pl.num_programs(1) + pl.program_id(1))


def ffn_up(hn, w1, w3, streams):
    tm, tn = TM_BIG, 512
    nj = DFF // tn
    wspec = pl.BlockSpec((D, tn), lambda i, j: (0, j))
    plan, s_in, s_out, s_shapes = _side_plan(streams, lambda i, j: i * nj + j, (M // tm) * nj)
    outs = pl.pallas_call(
        functools.partial(_ffn_up_kernel, plan=plan),
        grid=(M // tm, nj),
        in_specs=[pl.BlockSpec((tm, D), lambda i, j: (i, 0)), wspec, wspec] + s_in,
        out_specs=[pl.BlockSpec((tm, tn), lambda i, j: (i, j))] + s_out,
        out_shape=[jax.ShapeDtypeStruct((M, DFF), BF16)] + s_shapes,
        compiler_params=_cp("arbitrary", "arbitrary"),
        name="ffn_up",
    )(hn, w1, w3, *[s[0] for s in streams])
    return outs[0], outs[1:]


def _ffn_down_kernel(h_ref, w_ref, x_ref, g_ref, *refs, plan):
    n = len(plan)
    x2_ref, xn_ref = refs[n], refs[n + 1]
    x2 = x_ref[...] + jnp.dot(h_ref[...], w_ref[...], preferred_element_type=F32)
    x2_ref[...] = x2
    xn_ref[...] = _rms(x2, g_ref[...]).astype(xn_ref.dtype)
    _run_side(plan, refs[:n], refs[n + 2:], pl.program_id(0))


def ffn_down(h, w2, x1, g, streams):
    tm = TM_TOK
    plan, s_in, s_out, s_shapes = _side_plan(streams, lambda i: i, M // tm)
    outs = pl.pallas_call(
        functools.partial(_ffn_down_kernel, plan=plan),
        grid=(M // tm,),
        in_specs=[pl.BlockSpec((tm, DFF), lambda i: (i, 0)),
                  pl.BlockSpec((DFF, D), lambda i: (0, 0), pipeline_mode=pl.Buffered(1)),
                  pl.BlockSpec((tm, D), lambda i: (i, 0)), pl.BlockSpec((1, D), lambda i: (0, 0))] + s_in,
        out_specs=[pl.BlockSpec((tm, D), lambda i: (i, 0)), pl.BlockSpec((tm, D), lambda i: (i, 0))] + s_out,
        out_shape=[jax.ShapeDtypeStruct((M, D), F32), jax.ShapeDtypeStruct((M, D), BF16)] + s_shapes,
        compiler_params=_cp("arbitrary"),
        name="ffn_down",
    )(h, w2, x1, g.reshape(1, D), *[s[0] for s in streams])
    return outs[0], outs[1], outs[2:]


def _row_copy(src, src_row, dst, dst_row, sem):
    return pltpu.make_async_copy(src.at[pl.ds(src_row, 1)], dst.at[pl.ds(dst_row, 1)], sem)


def _moe_scatter_kernel(s1_ref, s2_ref, zt_ref, x_ref, g_ref, xs_ref, buf_ref, zero_ref, sem):
    i = pl.program_id(0)
    tm = MOE_TM
    slot = i % 2

    def tile_copy(e):
        return pltpu.make_async_copy(zero_ref, xs_ref.at[pl.ds(pl.multiple_of(zt_ref[e] * tm, tm), tm)],
                                     sem.at[0])

    @pl.when(i == 0)
    def _():
        zero_ref[...] = jnp.zeros(zero_ref.shape, F32)
        for e in range(2 * NE):
            tile_copy(e).start()
            tile_copy(e).wait()

    buf_ref[slot] = _rms(x_ref[...], g_ref[...])

    def issue(r, carry):
        t = i * tm + r
        _row_copy(buf_ref.at[slot], r, xs_ref, s1_ref[t], sem.at[slot]).start()
        _row_copy(buf_ref.at[slot], r, xs_ref, s2_ref[t], sem.at[slot]).start()
        return carry

    lax.fori_loop(0, tm, issue, 0, unroll=8)

    def drain_slot(sl):
        def drain(r, carry):
            _row_copy(buf_ref.at[sl], 0, xs_ref, 0, sem.at[sl]).wait()
            _row_copy(buf_ref.at[sl], 0, xs_ref, 0, sem.at[sl]).wait()
            return carry

        lax.fori_loop(0, tm, drain, 0, unroll=8)

    @pl.when(i > 0)
    def _():
        drain_slot(1 - slot)

    @pl.when(i == pl.num_programs(0) - 1)
    def _():
        drain_slot(slot)


def moe_scatter(x1, g, slot1, slot2, zero_tiles):
    tm = MOE_TM
    return pl.pallas_call(
        _moe_scatter_kernel,
        grid_spec=pltpu.PrefetchScalarGridSpec(
            num_scalar_prefetch=3,
            grid=(M // tm,),
            in_specs=[pl.BlockSpec((tm, D), lambda i, *_: (i, 0)), pl.BlockSpec((1, D), lambda i, *_: (0, 0))],
            out_specs=pl.BlockSpec(memory_space=pl.ANY),
            scratch_shapes=[pltpu.VMEM((2, tm, D), F32), pltpu.VMEM((tm, D), F32),
                            pltpu.SemaphoreType.DMA((2,))]),
        out_shape=jax.ShapeDtypeStruct((MOE_ROWS + 2 * MOE_SKIP, D), F32),
        compiler_params=_cp("arbitrary"),
        name="moe_scatter",
    )(slot1, slot2, zero_tiles, x1, g.reshape(1, D))


def _moe_up_kernel(te_ref, nu_ref, x_ref, w1_ref, w3_ref, o_ref):
    used = pl.program_id(1) < nu_ref[0]

    @pl.when(used)
    def _():
        x = x_ref[...].astype(BF16)
        a = jnp.dot(x, w1_ref[0], preferred_element_type=F32)
        b = jnp.dot(x, w3_ref[0], preferred_element_type=F32)
        o_ref[...] = (_silu(a) * b).astype(o_ref.dtype)

    @pl.when(jnp.logical_not(used))
    def _():
        o_ref[...] = jnp.zeros(o_ref.shape, o_ref.dtype)


def _used(i, nu_ref):
    return jnp.minimum(i, nu_ref[0] - 1)


def moe_up(xs, w1, w3, tile_expert, n_used, idx):
    tm, tn = MOE_TM, UP_TN
    wspec = pl.BlockSpec((None, 1, D, tn), lambda j, i, te, nu: (idx, te[_used(i, nu)], 0, j))
    return pl.pallas_call(
        _moe_up_kernel,
        grid_spec=pltpu.PrefetchScalarGridSpec(
            num_scalar_prefetch=2,
            grid=(DFF // tn, MOE_TILES),
            in_specs=[pl.BlockSpec((tm, D), lambda j, i, te, nu: (_used(i, nu), 0)), wspec, wspec],
            out_specs=pl.BlockSpec((tm, tn), lambda j, i, te, nu: (i, j))),
        out_shape=jax.ShapeDtypeStruct((MOE_ROWS, DFF), BF16),
        compiler_params=_cp("arbitrary", "arbitrary"),
        name="moe_up",
    )(tile_expert, n_used, xs, w1, w3)


def _moe_down_kernel(te_ref, nu_ref, h_ref, w_ref, o_ref):
    used = pl.program_id(1) < nu_ref[0]

    @pl.when(used)
    def _():
        o_ref[...] = jnp.dot(h_ref[...], w_ref[0], preferred_element_type=F32)

    @pl.when(jnp.logical_not(used))
    def _():
        o_ref[...] = jnp.zeros(o_ref.shape, o_ref.dtype)


def moe_down(hs, w2, tile_expert, n_used, idx):
    tm, tn = MOE_TM, DOWN_TN
    return pl.pallas_call(
        _moe_down_kernel,
        grid_spec=pltpu.PrefetchScalarGridSpec(
            num_scalar_prefetch=2,
            grid=(D // tn, MOE_TILES),
            in_specs=[pl.BlockSpec((tm, DFF), lambda j, i, te, nu: (_used(i, nu), 0)),
                      pl.BlockSpec((None, 1, DFF, tn), lambda j, i, te, nu: (idx, te[_used(i, nu)], 0, j))],
            out_specs=pl.BlockSpec((tm, tn), lambda j, i, te, nu: (i, j))),
        out_shape=jax.ShapeDtypeStruct((MOE_ROWS, D), F32),
        compiler_params=_cp("arbitrary", "arbitrary"),
        name="moe_down",
    )(tile_expert, n_used, hs, w2)


COMB_TM = BLK


def _moe_combine_kernel(s1_ref, s2_ref, x_ref, route_ref, g_ref, ys_ref, yp_ref, ysm_ref, ya_ref, yb_ref, sem):
    i = pl.program_id(0)
    tm = COMB_TM
    slot = i % 2

    def fetch(tile, sl):
        def issue(r, carry):
            t = tile * tm + r
            _row_copy(ys_ref, s1_ref[t], ya_ref.at[sl], r, sem.at[sl]).start()
            _row_copy(ys_ref, s2_ref[t], yb_ref.at[sl], r, sem.at[sl]).start()
            return carry

        lax.fori_loop(0, tm, issue, 0, unroll=8)

    @pl.when(i == 0)
    def _():
        fetch(0, 0)

    @pl.when(i + 1 < pl.num_programs(0))
    def _():
        fetch(i + 1, 1 - slot)

    def drain(r, carry):
        _row_copy(ys_ref, 0, ya_ref.at[slot], 0, sem.at[slot]).wait()
        _row_copy(ys_ref, 0, yb_ref.at[slot], 0, sem.at[slot]).wait()
        return carry

    lax.fori_loop(0, tm, drain, 0, unroll=8)
    w1 = route_ref[:, 2:3]
    w2 = route_ref[:, 3:4]
    x2 = x_ref[...] + (w1 * ya_ref[slot] + w2 * yb_ref[slot])
    y = _rms(x2, g_ref[...])
    is_prompt = i < B * NB

    @pl.when(is_prompt & (i % NB > 0))
    def _():
        yp_ref[...] = y

    @pl.when(jnp.logical_not(is_prompt))
    def _():
        ysm_ref[...] = y


def moe_combine(x1, route, g, ys, slot1, slot2):
    tm = COMB_TM
    return pl.pallas_call(
        _moe_combine_kernel,
        grid_spec=pltpu.PrefetchScalarGridSpec(
            num_scalar_prefetch=2,
            grid=(M // tm,),
            in_specs=[pl.BlockSpec((tm, D), lambda i, *_: (i, 0)), pl.BlockSpec((tm, ROUTE_W), lambda i, *_: (i, 0)),
                      pl.BlockSpec((1, D), lambda i, *_: (0, 0)), pl.BlockSpec(memory_space=pl.ANY)],
            out_specs=[pl.BlockSpec((tm, D), lambda i, *_: (_prompt_block(i), 0)),
                       pl.BlockSpec((tm, D), lambda i, *_: (jnp.maximum(i - B * NB, 0), 0))],
            scratch_shapes=[pltpu.VMEM((2, tm, D), F32), pltpu.VMEM((2, tm, D), F32),
                            pltpu.SemaphoreType.DMA((2,))]),
        out_shape=[jax.ShapeDtypeStruct((B * SEQ, D), F32), jax.ShapeDtypeStruct((MS, D), F32)],
        compiler_params=_cp("arbitrary"),
        name="moe_combine",
    )(slot1, slot2, x1, route, g.reshape(1, D), ys)


def _moe_plan(route):
    e1 = route[:, 0].astype(jnp.int32)
    e2 = route[:, 1].astype(jnp.int32)
    ids = jnp.arange(NE, dtype=jnp.int32)
    r = jnp.arange(M, dtype=jnp.int32)
    skip = (r < MP) & (r % PB < BLK)
    spare = (r // PB) * BLK + r % PB
    sel = (e1[:, None] == ids).astype(jnp.int32) + (e2[:, None] == ids).astype(jnp.int32)
    sel = jnp.where(skip[:, None], 0, sel)
    incl = jnp.cumsum(sel, axis=0)
    rank = incl - sel
    cnt = incl[-1]
    ntile = (cnt + MOE_TM - 1) // MOE_TM
    tile_end = jnp.cumsum(ntile)
    tile_off = tile_end - ntile
    row_off = tile_off * MOE_TM
    slot1 = jnp.take(row_off, e1) + jnp.take_along_axis(rank, e1[:, None], axis=1)[:, 0]
    slot2 = jnp.take(row_off, e2) + jnp.take_along_axis(rank, e2[:, None], axis=1)[:, 0]
    tiles = jnp.arange(MOE_TILES, dtype=jnp.int32)
    tile_expert = jnp.minimum(jnp.sum((tiles[:, None] >= tile_end[None, :]).astype(jnp.int32), axis=1), NE - 1)
    n_used = tile_end[-1:].astype(jnp.int32)
    tail = jnp.minimum(tile_end[-1] + ids, MOE_TILES - 1)
    zero_tiles = jnp.concatenate([jnp.maximum(tile_end - 1, 0), tail]).astype(jnp.int32)
    scatter = (jnp.where(skip, MOE_ROWS + spare, slot1).astype(jnp.int32),
               jnp.where(skip, MOE_ROWS + MOE_SKIP + spare, slot2).astype(jnp.int32))
    gather = (jnp.where(skip, spare, slot1).astype(jnp.int32), jnp.where(skip, spare, slot2).astype(jnp.int32))
    return scatter, gather, tile_expert.astype(jnp.int32), n_used, zero_tiles


def _blockdiag(w):
    per = GATE_CH // (LW // 16)
    w4 = w.reshape(LW // GATE_CH, per, 64, 64)
    eye = jnp.eye(per, dtype=w.dtype)
    return jnp.einsum("cnde,nm->cndme", w4, eye).reshape(LW // GATE_CH, GATE_CH, GATE_CH).astype(BF16)


def kernel(x_prompt, x_sample, cache_attn_k, cache_attn_v, state_conv, state_lru_conv, state_lru_h, meta_tokens,
           norm_mix, norm_ffn, norm_final, w_in, attn_sinks, w_attn_out, conv_dw_w, conv_dw_b, conv_ln_g, conv_ln_b,
           w_conv_out, lru_conv_w, lru_conv_b, lru_w_r, lru_b_r, lru_w_i, lru_b_i, lru_lambda, w_lru_out, w_mix_out,
           ffn_w1, ffn_w3, ffn_w2, moe_router_w, moe_router_b, moe_w1, moe_w3, moe_w2):
    states_in = (cache_attn_k.reshape(DEPTH * DB, WIN, KVW), cache_attn_v.reshape(DEPTH * DB, WIN, KVW),
                 state_conv.reshape(DEPTH * DB, CK - 1, CW), state_lru_conv.reshape(DEPTH * DB, LCK - 1, LW),
                 state_lru_h.reshape(DEPTH * DB, LW))
    states_out = tuple(jnp.zeros(s.shape, F32) for s in states_in)
    p_k, p_v, p_conv, p_lconv, p_h = [], [], [], [], []

    assert (DEPTH, ffn_w1.shape[0], moe_w1.shape[0]) == (2, 1, 1), \
        "layer 0: dense SwiGLU; layer 1: routed experts followed by the final norm"
    wa, wc, wl, wm = (w.astype(BF16) for w in (w_attn_out, w_conv_out, w_lru_out, w_mix_out))
    w_in0 = w_in[0].astype(BF16)

    def branches(l, z):
        nonlocal states_out
        sinks = attn_sinks[l]
        conv_p = (conv_dw_w[l], conv_dw_b[l], conv_ln_g[l], conv_ln_b[l])
        lru_p = (lru_conv_w[l], lru_conv_b[l], _blockdiag(lru_w_r[l]), lru_b_r[l], _blockdiag(lru_w_i[l]),
                 lru_b_i[l], lru_lambda[l])
        o_p, c_p, rr_p, pc, plc, ph = prompt_branches(z, sinks, conv_p, lru_p)
        o_s, c_s, rr_s, states_out = sample_branches(z, sinks, conv_p, lru_p, states_in, states_out, l)
        kv_last = jnp.stack([lax.slice(z, ((b + 1) * PB - WIN, Z_K), ((b + 1) * PB, Z_K + 2 * KVW))
                             for b in range(B)]).astype(F32)
        p_k.append(kv_last[:, :, :KVW].reshape(B, WIN, NKV, HD))
        p_v.append(kv_last[:, :, KVW:].reshape(B, WIN, NKV, HD))
        p_conv.append(pc)
        p_lconv.append(plc)
        p_h.append(ph.reshape(B, LW))
        return (o_p, o_s), (c_p, c_s), (rr_p, rr_s)

    x, xn = embed_norm(x_prompt, x_sample, meta_tokens, norm_mix[0])

    z, (moe_w1_b, ffn_w1_b, ffn_w3_b, ffn_w2_b) = inproj(
        xn, w_in0, [(moe_w1.reshape(-1, DFF), 128, None), (ffn_w1, 128, 0), (ffn_w3, 128, 0), (ffn_w2, 256, 0)])
    o, c, rr = branches(0, z)
    x1, hn = mix(o, c, rr, z, x, wa, wc, wl, wm, norm_ffn[0], 0)
    h, (moe_w2_b,) = ffn_up(hn, ffn_w1_b, ffn_w3_b, [(moe_w2.reshape(-1, D), 512, None)])
    x, xn, (w_in1,) = ffn_down(h, ffn_w2_b, x1, norm_mix[1], [(w_in, 64, 1)])

    z, (moe_w3_b,) = inproj(xn, w_in1, [(moe_w3.reshape(-1, DFF), 128, None)])
    o, c, rr = branches(1, z)
    rw = jnp.pad(moe_router_w[0], ((0, 0), (0, ROUTE_W - NE)))
    rw_hi = rw.astype(BF16)
    rw_lo = (rw - rw_hi.astype(F32)).astype(BF16)
    rb = jnp.pad(moe_router_b[0].reshape(1, NE), ((0, 0), (0, ROUTE_W - NE)))
    x1, route = mix(o, c, rr, z, x, wa, wc, wl, wm, norm_ffn[1], 1,
                    router=(jnp.concatenate([rw_hi, rw_lo], axis=1), rb))
    scatter_slots, gather_slots, tile_expert, n_used, zero_tiles = _moe_plan(route)
    xs = moe_scatter(x1, norm_ffn[1], *scatter_slots, zero_tiles)
    hs = moe_up(xs, moe_w1_b.reshape(moe_w1.shape), moe_w3_b.reshape(moe_w3.shape), tile_expert, n_used, 0)
    ys = moe_down(hs, moe_w2_b.reshape(moe_w2.shape), tile_expert, n_used, 0)
    y_prompt, y_sample = moe_combine(x1, route, norm_final, ys, *gather_slots)

    y_prompt = y_prompt.reshape(B, SEQ, D)
    y_sample = y_sample.reshape(DB, T, D)
    s_k, s_v, s_conv, s_lconv, s_h = states_out
    st4 = lambda a: a.reshape(DEPTH, DB, WIN, NKV, HD)
    return (y_prompt, y_sample,
            jnp.stack(p_k), jnp.stack(p_v), jnp.stack(p_conv), jnp.stack(p_lconv), jnp.stack(p_h),
            st4(s_k), st4(s_v), s_conv.reshape(DEPTH, DB, CK - 1, CW), s_lconv.reshape(DEPTH, DB, LCK - 1, LW),
            s_h.reshape(DEPTH, DB, LW))
```

```python
import functools
import math

import jax
import jax.numpy as jnp
from jax import lax
from jax.experimental import pallas as pl
from jax.experimental.pallas import tpu as pltpu

F32 = jnp.float32
BF16 = jnp.bfloat16

D = 2048
B = 2
SEQ = 4096
DEPTH = 2
DB = 128
T = 8
PAST = 8192
N_META = 16
NH = 16
NKV = 4
G = NH // NKV
HD = 64
AW = NH * HD
KVW = NKV * HD
WIN = 128
BLK = 128
CW = D // 2
CK = 31
LW = D // 2
LCK = 4
LRU_C = 8.0
DFF = 5632
NE = 8
EPS = 1e-6
NEG = -1e30
IN_W = AW + 2 * KVW + 2 * CW + 2 * LW + 3 * D

PAD = (-N_META) % BLK
PB = PAD + N_META + SEQ
NB = PB // BLK
MP = B * PB
MS = DB * T
M = MP + MS

ZC = 512
NZC = IN_W // ZC
Z_Q, Z_A, Z_GATE, Z_LX, Z_LG, Z_GA, Z_K, Z_V = 0, 1024, 2048, 3072, 4096, 5120, 11264, 11520

SLOPES = tuple(2.0 ** (-8.0 * (h + 1.0) / NH) for h in range(NH))

TM_BIG = 1184
TM_MID = 592
TM_TOK = 256
SB = 8
MOE_TM = 256
MOE_SKIP = B * BLK
MOE_TILES = (2 * (M - MOE_SKIP)) // MOE_TM + NE
MOE_ROWS = MOE_TILES * MOE_TM
UP_TN = 1408
DOWN_TN = 1024

VMEM_LIMIT = 56 * 1024 * 1024


def _cp(*sem):
    return pltpu.CompilerParams(dimension_semantics=sem, vmem_limit_bytes=VMEM_LIMIT)


def _rms(x, g):
    return x * lax.rsqrt(jnp.mean(x * x, axis=-1, keepdims=True) + EPS) * g


def _sigmoid(x):
    return 1.0 / (1.0 + jnp.exp(-x))


def _silu(x):
    return x * _sigmoid(x)


def _prompt_block(i):
    ic = jnp.minimum(i, B * NB - 1)
    return (ic // NB) * (NB - 1) + jnp.maximum(ic % NB - 1, 0)


def _embed_norm_kernel(xp_ref, xs_ref, meta_ref, g_ref, x_ref, xn_ref):
    i = pl.program_id(0)
    head = jnp.concatenate([jnp.zeros((PAD, D), F32), meta_ref[...]], axis=0)
    x = jnp.where(i % NB == 0, head, xp_ref[...])
    x = jnp.where(i < B * NB, x, xs_ref[...])
    x_ref[...] = x
    xn_ref[...] = _rms(x, g_ref[...]).astype(xn_ref.dtype)


def embed_norm(x_prompt, x_sample, meta, g):
    tm = BLK
    return pl.pallas_call(
        _embed_norm_kernel,
        grid=(M // tm,),
        in_specs=[pl.BlockSpec((tm, D), lambda i: (_prompt_block(i), 0)),
                  pl.BlockSpec((tm, D), lambda i: (jnp.maximum(i - B * NB, 0), 0)),
                  pl.BlockSpec((N_META, D), lambda i: (0, 0)), pl.BlockSpec((1, D), lambda i: (0, 0))],
        out_specs=[pl.BlockSpec((tm, D), lambda i: (i, 0)), pl.BlockSpec((tm, D), lambda i: (i, 0))],
        out_shape=[jax.ShapeDtypeStruct((M, D), F32), jax.ShapeDtypeStruct((M, D), BF16)],
        compiler_params=_cp("parallel"),
        name="embed_norm",
    )(x_prompt.reshape(B * SEQ, D), x_sample.reshape(MS, D), meta, g.reshape(1, D))


def _side_plan(streams, step_of, total_steps):
    plan, in_specs, out_specs, out_shapes = [], [], [], []
    first = 0
    for arr, rows, lead in streams:
        nrows, ncols = arr.shape[-2:]
        nblk = nrows // rows
        assert nblk * rows == nrows

        def block(*g, first=first, nblk=nblk):
            return jnp.clip(step_of(*g) - first, 0, nblk - 1)

        if lead is None:
            in_specs.append(pl.BlockSpec((rows, ncols), lambda *g, block=block: (block(*g), 0)))
        else:
            in_specs.append(pl.BlockSpec((None, rows, ncols), lambda *g, block=block, lead=lead: (lead, block(*g), 0)))
        out_specs.append(pl.BlockSpec((rows, ncols), lambda *g, block=block: (block(*g), 0)))
        out_shapes.append(jax.ShapeDtypeStruct((nrows, ncols), BF16))
        plan.append((first, nblk))
        first += nblk
    assert first <= total_steps
    return plan, in_specs, out_specs, out_shapes


def _run_side(plan, src_refs, dst_refs, step):
    for (first, nblk), src_ref, dst_ref in zip(plan, src_refs, dst_refs):
        @pl.when((step >= first) & (step < first + nblk))
        def _():
            dst_ref[...] = src_ref[...].astype(dst_ref.dtype)


def _inproj_kernel(x_ref, w_ref, *refs, plan):
    n = len(plan)
    o_ref = refs[n]
    o_ref[...] = jnp.dot(x_ref[...], w_ref[...], preferred_element_type=F32).astype(o_ref.dtype)
    _run_side(plan, refs[:n], refs[n + 1:], pl.program_id(0) * NZC + pl.program_id(1))


def _z_src_block(j):
    return jnp.where(j < 2, j, jnp.where(j < NZC - 1, j + 1, 2))


def inproj(xn, w, streams, tm=TM_BIG):
    steps = (M // tm) * NZC
    plan, s_in, s_out, s_shapes = _side_plan(streams, lambda i, j: i * NZC + j, steps)
    outs = pl.pallas_call(
        functools.partial(_inproj_kernel, plan=plan),
        grid=(M // tm, NZC),
        in_specs=[pl.BlockSpec((tm, D), lambda i, j: (i, 0)),
                  pl.BlockSpec((D, ZC), lambda i, j: (0, _z_src_block(j)))] + s_in,
        out_specs=[pl.BlockSpec((tm, ZC), lambda i, j: (i, j))] + s_out,
        out_shape=[jax.ShapeDtypeStruct((M, IN_W), BF16)] + s_shapes,
        compiler_params=_cp("arbitrary", "arbitrary"),
        name="inproj",
    )(xn, w, *[s[0] for s in streams])
    return outs[0], outs[1:]


def _softmax_sink(s, sink):
    m = jnp.maximum(jnp.max(s, axis=-1, keepdims=True), sink)
    e = jnp.exp(s - m)
    l = jnp.sum(e, axis=-1, keepdims=True) + jnp.exp(sink - m)
    return e / l


def _attn_prompt_start(n, bias_ref):
    @pl.when(n == 0)
    def _():
        r = lax.broadcasted_iota(jnp.int32, (BLK, 2 * BLK), 0)
        c = lax.broadcasted_iota(jnp.int32, (BLK, 2 * BLK), 1)
        dist = BLK + r - c
        in_window = (dist >= 0) & (dist <= WIN)
        distf = dist.astype(F32)
        for hq in range(NH):
            bias_ref[hq] = jnp.where(in_window, -SLOPES[hq] * distf, NEG)


def _attn_prompt_main(n, sink_ref, q_ref, kp_ref, kc_ref, vp_ref, vc_ref, o_ref, bias_ref):
    c = lax.broadcasted_iota(jnp.int32, (BLK, 2 * BLK), 1)
    exists = (n - 1) * BLK - PAD + c >= 0
    q = q_ref[...] * (HD ** -0.5)
    k = jnp.concatenate([kp_ref[...], kc_ref[...]], axis=0)
    v = jnp.concatenate([vp_ref[...], vc_ref[...]], axis=0)
    for h in range(NKV):
        kh = k[:, h * HD:(h + 1) * HD]
        vh = v[:, h * HD:(h + 1) * HD]
        qh = jnp.concatenate([q[:, (h * G + g) * HD:(h * G + g + 1) * HD] for g in range(G)], axis=0)
        s = lax.dot_general(qh, kh, (((1,), (1,)), ((), ())), preferred_element_type=F32)
        ps = []
        for g in range(G):
            hq = h * G + g
            sg = jnp.where(exists, s[g * BLK:(g + 1) * BLK] + bias_ref[hq], NEG)
            ps.append(_softmax_sink(sg, sink_ref[hq]).astype(BF16))
        oh = jnp.dot(jnp.concatenate(ps, axis=0), vh, preferred_element_type=F32)
        for g in range(G):
            hq = h * G + g
            o_ref[:, hq * HD:(hq + 1) * HD] = oh[g * BLK:(g + 1) * BLK].astype(o_ref.dtype)


def _attn_sample_body(sink_ref, q_ref, kn_ref, vn_ref, ck_ref, cv_ref, o_ref, nk_ref, nv_ref):
    q = q_ref[...].astype(F32).reshape(SB, T, AW)
    kn = kn_ref[...].astype(F32).reshape(SB, T, KVW)
    vn = vn_ref[...].astype(F32).reshape(SB, T, KVW)
    ck = ck_ref[...]
    cv = cv_ref[...]
    wb = WIN
    nk_ref[:, 0:wb - T, :] = ck[:, T:wb, :]
    nk_ref[:, wb - T:wb, :] = kn
    nv_ref[:, 0:wb - T, :] = cv[:, T:wb, :]
    nv_ref[:, wb - T:wb, :] = vn
    zpad = jnp.zeros((SB, wb - T, KVW), F32)
    k = jnp.concatenate([ck, kn, zpad], axis=1).astype(BF16)
    v = jnp.concatenate([cv, vn, zpad], axis=1).astype(BF16)
    nkeys = 2 * wb
    r = lax.broadcasted_iota(jnp.int32, (G * T, nkeys), 0)
    c = lax.broadcasted_iota(jnp.int32, (G * T, nkeys), 1)
    dist = wb + (r % T) - c
    allowed = (dist >= 0) & (dist <= WIN)
    distf = dist.astype(F32)
    gidx = r // T
    for h in range(NKV):
        slope = jnp.zeros((G * T, nkeys), F32)
        sink = jnp.zeros((G * T, 1), F32)
        for g in range(G):
            slope = jnp.where(gidx == g, SLOPES[h * G + g], slope)
            sink = jnp.where(gidx[:, 0:1] == g, sink_ref[h * G + g], sink)
        kh = k[:, :, h * HD:(h + 1) * HD]
        vh = v[:, :, h * HD:(h + 1) * HD]
        qh = jnp.concatenate([q[:, :, (h * G + g) * HD:(h * G + g + 1) * HD] for g in range(G)],
                             axis=1).astype(BF16)
        s = jnp.einsum("bqd,bkd->bqk", qh, kh, preferred_element_type=F32)
        s = s * (HD ** -0.5) - (slope * distf)[None]
        s = jnp.where(allowed[None], s, NEG)
        p = _softmax_sink(s, sink[None]).astype(BF16)
        oh = jnp.einsum("bqk,bkd->bqd", p, vh, preferred_element_type=F32)
        for g in range(G):
            hq = h * G + g
            piece = oh[:, g * T:(g + 1) * T, :].reshape(SB * T, HD)
            o_ref[:, hq * HD:(hq + 1) * HD] = piece.astype(o_ref.dtype)


def _ln_swish(y, g, b):
    mu = jnp.mean(y, axis=-1, keepdims=True)
    yc = y - mu
    var = jnp.mean(yc * yc, axis=-1, keepdims=True)
    yn = yc * lax.rsqrt(var + EPS) * g + b
    return _silu(yn)


CONV_HIST = 32
SUBLANES = 8


def _dwconv_by_phase(load, wrow, nrows, first, ntaps):
    y = None
    for s in range(SUBLANES):
        n = nrows if s == 0 else nrows + SUBLANES
        part = None
        for q in range((first + ntaps - 1) // SUBLANES + 1):
            o = SUBLANES * q + s
            if first <= o < first + ntaps:
                term = wrow(o - first) * load(SUBLANES * q, n)
                part = term if part is None else part + term
        if part is not None:
            part = part[s:s + nrows]
            y = part if y is None else y + part
    return y


def _conv_prompt_start(n, ext_ref):
    @pl.when(n == 0)
    def _():
        ext_ref[0:CONV_HIST, :] = jnp.zeros((CONV_HIST, CW), F32)

    @pl.when(n > 0)
    def _():
        ext_ref[0:CONV_HIST, :] = ext_ref[BLK:BLK + CONV_HIST, :]


def _conv_prompt_main(n, a_ref, g_ref, w_ref, b_ref, lg_ref, lb_ref, c_ref, ext_ref, y_ref):
    u = a_ref[...].astype(F32) * _sigmoid(g_ref[...].astype(F32))
    row = lax.broadcasted_iota(jnp.int32, (BLK, 1), 0)
    u = jnp.where((n > 0) | (row >= PAD), u, 0.0)
    ext_ref[CONV_HIST:CONV_HIST + BLK, :] = u
    off = CONV_HIST - (CK - 1)
    cc = 256
    for c0 in range(0, CW, cc):
        cs = slice(c0, c0 + cc)
        y_ref[:, cs] = b_ref[:, cs] + _dwconv_by_phase(lambda r0, nr: ext_ref[r0:r0 + nr, cs],
                                                      lambda j: w_ref[j:j + 1, cs], BLK, off, CK)
    c_ref[...] = _ln_swish(y_ref[...], lg_ref[...], lb_ref[...]).astype(c_ref.dtype)


def _conv_prompt_finish(n, st_ref, ext_ref):
    @pl.when(n == NB - 1)
    def _():
        st_ref[0] = ext_ref[CONV_HIST + BLK - (CK - 1):CONV_HIST + BLK, :]


def _conv_sample_body(a_ref, g_ref, st_ref, w_ref, b_ref, lg_ref, lb_ref, c_ref, nst_ref, ext_ref, y_ref):
    u = a_ref[...].astype(F32) * _sigmoid(g_ref[...].astype(F32))
    hist = CK - 1
    base = CONV_HIST - hist
    for s in range(SB):
        ext_ref[s, 0:base, :] = jnp.zeros((base, CW), F32)
        ext_ref[s, base:CONV_HIST, :] = st_ref[s]
        ext_ref[s, CONV_HIST:CONV_HIST + T, :] = u[s * T:(s + 1) * T]
    for s in range(SB):
        y_ref[s * T:(s + 1) * T, :] = b_ref[...] + _dwconv_by_phase(
            lambda r0, nr: ext_ref[s, r0:r0 + nr, :], lambda j: w_ref[j:j + 1, :], T, base, CK)
        nst_ref[s] = ext_ref[s, base + T:base + T + hist, :]
    c_ref[...] = _ln_swish(y_ref[...], lg_ref[...], lb_ref[...]).astype(c_ref.dtype)


LRU_HIST = 8
GATE_CH = 256


def _gelu_tanh(x):
    return x * (0.5 * (1.0 + jnp.tanh(math.sqrt(2.0 / math.pi) * (x + 0.044715 * (x * x * x)))))


def _softplus(x):
    return jnp.maximum(x, 0.0) + jnp.log1p(jnp.exp(-jnp.abs(x)))


def _expm1(x):
    return jnp.tanh(0.5 * x) * (jnp.exp(x) + 1.0)


def _lru_gates(xc, wr_ref, br_ref, wi_ref, bi_ref, lam_ref, valid, a_ref, bx_ref):
    sp = _softplus(-lam_ref[...])
    for k in range(LW // GATE_CH):
        sl = slice(k * GATE_CH, (k + 1) * GATE_CH)
        xk = xc[:, sl]
        xkb = xk.astype(BF16)
        r = _sigmoid(jnp.dot(xkb, wr_ref[k], preferred_element_type=F32) + br_ref[:, sl])
        i = _sigmoid(jnp.dot(xkb, wi_ref[k], preferred_element_type=F32) + bi_ref[:, sl])
        log_a = (-LRU_C) * r * sp[:, sl]
        a = jnp.exp(log_a)
        bx = jnp.sqrt(-_expm1(2.0 * log_a)) * (i * xk)
        if valid is not None:
            bx = jnp.where(valid, bx, 0.0)
        a_ref[:, sl] = a
        bx_ref[:, sl] = bx


def _lru_prompt_start(n, ext_ref, carry_ref):
    @pl.when(n == 0)
    def _():
        ext_ref[0:LRU_HIST, :] = jnp.zeros((LRU_HIST, LW), F32)
        carry_ref[...] = jnp.zeros((1, LW), F32)

    @pl.when(n > 0)
    def _():
        ext_ref[0:LRU_HIST, :] = ext_ref[BLK:BLK + LRU_HIST, :]


def _lru_prompt_main(n, x_ref, g_ref, cw_ref, cb_ref, wr_ref, br_ref, wi_ref, bi_ref, lam_ref,
                     rr_ref, ext_ref, a_ref, bx_ref, h_ref, carry_ref):
    row = lax.broadcasted_iota(jnp.int32, (BLK, 1), 0)
    valid = (n > 0) | (row >= PAD)
    ext_ref[LRU_HIST:LRU_HIST + BLK, :] = jnp.where(valid, x_ref[...].astype(F32), 0.0)
    off = LRU_HIST - (LCK - 1)
    xc = jnp.zeros((BLK, LW), F32) + cb_ref[...]
    for j in range(LCK):
        xc = xc + cw_ref[j:j + 1, :] * ext_ref[off + j:off + j + BLK, :]
    _lru_gates(xc, wr_ref, br_ref, wi_ref, bi_ref, lam_ref, valid, a_ref, bx_ref)

    h = carry_ref[...]
    for t in range(BLK):
        h = a_ref[t:t + 1, :] * h + bx_ref[t:t + 1, :]
        h_ref[t:t + 1, :] = h
    carry_ref[...] = h
    rr_ref[...] = (h_ref[...] * _gelu_tanh(g_ref[...].astype(F32))).astype(rr_ref.dtype)


def _lru_prompt_finish(n, cst_ref, hst_ref, ext_ref, carry_ref):
    @pl.when(n == NB - 1)
    def _():
        cst_ref[0] = ext_ref[LRU_HIST + BLK - (LCK - 1):LRU_HIST + BLK, :]
        hst_ref[0] = carry_ref[...]


def _prompt_branches_kernel(sink_ref, q_ref, kp_ref, kc_ref, vp_ref, vc_ref, ca_ref, cg_ref, lx_ref, lg_ref,
                            dw_ref, db_ref, lng_ref, lnb_ref,
                            cw_ref, cb_ref, wr_ref, br_ref, wi_ref, bi_ref, lam_ref,
                            o_ref, c_ref, rr_ref, cst_ref, lcst_ref, hst_ref,
                            bias_ref, cext_ref, cy_ref, lext_ref, a_ref, bx_ref, h_ref, carry_ref):
    n = pl.program_id(1)
    _attn_prompt_start(n, bias_ref)
    _conv_prompt_start(n, cext_ref)
    _lru_prompt_start(n, lext_ref, carry_ref)
    _attn_prompt_main(n, sink_ref, q_ref, kp_ref, kc_ref, vp_ref, vc_ref, o_ref, bias_ref)
    _conv_prompt_main(n, ca_ref, cg_ref, dw_ref, db_ref, lng_ref, lnb_ref, c_ref, cext_ref, cy_ref)
    _lru_prompt_main(n, lx_ref, lg_ref, cw_ref, cb_ref, wr_ref, br_ref, wi_ref, bi_ref, lam_ref,
                     rr_ref, lext_ref, a_ref, bx_ref, h_ref, carry_ref)
    _conv_prompt_finish(n, cst_ref, cext_ref)
    _lru_prompt_finish(n, lcst_ref, hst_ref, lext_ref, carry_ref)


def prompt_branches(z, sinks, conv_p, lru_p):
    kb, vb = Z_K // KVW, Z_V // KVW
    cur = lambda col: (lambda b, n: (b * NB + n, col))
    prev = lambda col: (lambda b, n: (b * NB + jnp.maximum(n - 1, 0), col))
    full = lambda shape: pl.BlockSpec(shape, lambda b, n: (0,) * len(shape))
    wide = lambda col: pl.BlockSpec((BLK, 1024), cur(col))
    vec = full((1, 1024))
    wspec = full((LW // GATE_CH, GATE_CH, GATE_CH))
    dw_w, dw_b, ln_g, ln_b = conv_p
    cw, cb, wr, br, wi, bi, lam = lru_p
    row = lambda v: v.reshape(1, -1)
    blk = pltpu.VMEM((BLK, 1024), F32)
    return pl.pallas_call(
        _prompt_branches_kernel,
        grid=(B, NB),
        in_specs=[pl.BlockSpec(memory_space=pltpu.SMEM),
                  wide(0),
                  pl.BlockSpec((BLK, KVW), prev(kb)), pl.BlockSpec((BLK, KVW), cur(kb)),
                  pl.BlockSpec((BLK, KVW), prev(vb)), pl.BlockSpec((BLK, KVW), cur(vb)),
                  wide(Z_A // 1024), wide(Z_GATE // 1024), wide(Z_LX // 1024), wide(Z_LG // 1024),
                  full((CK, CW)), vec, vec, vec,
                  full((LCK, LW)), vec, wspec, vec, wspec, vec, vec],
        out_specs=[wide(0), wide(0), wide(0),
                   pl.BlockSpec((1, CK - 1, CW), lambda b, n: (b, 0, 0)),
                   pl.BlockSpec((1, LCK - 1, LW), lambda b, n: (b, 0, 0)),
                   pl.BlockSpec((1, 1, LW), lambda b, n: (b, 0, 0))],
        out_shape=[jax.ShapeDtypeStruct((MP, AW), BF16), jax.ShapeDtypeStruct((MP, CW), BF16),
                   jax.ShapeDtypeStruct((MP, LW), BF16), jax.ShapeDtypeStruct((B, CK - 1, CW), F32),
                   jax.ShapeDtypeStruct((B, LCK - 1, LW), F32), jax.ShapeDtypeStruct((B, 1, LW), F32)],
        scratch_shapes=[pltpu.VMEM((NH, BLK, 2 * BLK), F32),
                        pltpu.VMEM((CONV_HIST + BLK, CW), F32), blk,
                        pltpu.VMEM((LRU_HIST + BLK, LW), F32), blk, blk, blk, pltpu.VMEM((1, LW), F32)],
        compiler_params=_cp("parallel", "arbitrary"),
        name="prompt_branches",
    )(sinks, z, z, z, z, z, z, z, z, z, dw_w, row(dw_b), row(ln_g), row(ln_b),
      cw, row(cb), wr, row(br), wi, row(bi), row(lam))


def _lru_sample_body(x_ref, g_ref, cst_ref, h0_ref, cw_ref, cb_ref, wr_ref, br_ref, wi_ref, bi_ref, lam_ref,
                     rr_ref, ncst_ref, nh_ref, ext_ref, xc_ref, a_ref, bx_ref, h_ref):
    hist = LCK - 1
    base = LRU_HIST - hist
    x = x_ref[...].astype(F32)
    for s in range(SB):
        ext_ref[s, base:LRU_HIST, :] = cst_ref[s]
        ext_ref[s, LRU_HIST:LRU_HIST + T, :] = x[s * T:(s + 1) * T]
    for s in range(SB):
        acc = jnp.zeros((T, LW), F32) + cb_ref[...]
        for j in range(LCK):
            acc = acc + cw_ref[j:j + 1, :] * ext_ref[s, base + j:base + j + T, :]
        xc_ref[s * T:(s + 1) * T, :] = acc
        ncst_ref[s] = ext_ref[s, base + T:base + T + hist, :]
    _lru_gates(xc_ref[...], wr_ref, br_ref, wi_ref, bi_ref, lam_ref, None, a_ref, bx_ref)
    for s in range(SB):
        h = h0_ref[s:s + 1, :]
        for t in range(T):
            rw = s * T + t
            h = a_ref[rw:rw + 1, :] * h + bx_ref[rw:rw + 1, :]
            h_ref[rw:rw + 1, :] = h
        nh_ref[s:s + 1, :] = h
    rr_ref[...] = (h_ref[...] * _gelu_tanh(g_ref[...].astype(F32))).astype(rr_ref.dtype)


def _sample_branches_kernel(sink_ref, q_ref, kn_ref, vn_ref, ck_ref, cv_ref, ca_ref, cg_ref, cst_ref,
                            dw_ref, db_ref, lng_ref, lnb_ref, lx_ref, lg_ref, lcst_ref, h0_ref,
                            cw_ref, cb_ref, wr_ref, br_ref, wi_ref, bi_ref, lam_ref,
                            nk_in, nv_in, nst_in, ncst_in, nh_in,
                            o_ref, nk_ref, nv_ref, c_ref, nst_ref, rr_ref, ncst_ref, nh_ref,
                            cext_ref, cy_ref, lext_ref, xc_ref, a_ref, bx_ref, h_ref):
    del nk_in, nv_in, nst_in, ncst_in, nh_in
    _attn_sample_body(sink_ref, q_ref, kn_ref, vn_ref, ck_ref, cv_ref, o_ref, nk_ref, nv_ref)
    _conv_sample_body(ca_ref, cg_ref, cst_ref, dw_ref, db_ref, lng_ref, lnb_ref, c_ref, nst_ref, cext_ref, cy_ref)
    _lru_sample_body(lx_ref, lg_ref, lcst_ref, h0_ref, cw_ref, cb_ref, wr_ref, br_ref, wi_ref, bi_ref, lam_ref,
                     rr_ref, ncst_ref, nh_ref, lext_ref, xc_ref, a_ref, bx_ref, h_ref)


def sample_branches(z, sinks, conv_p, lru_p, states_in, states_out, layer):
    rows = SB * T
    rb0 = MP // rows
    sb0 = layer * (DB // SB)
    kb, vb = Z_K // KVW, Z_V // KVW
    zrow = lambda w, col: pl.BlockSpec((rows, w), lambda i: (rb0 + i, col))
    full = lambda shape: pl.BlockSpec(shape, lambda i: (0,) * len(shape))
    st3 = lambda n, w: pl.BlockSpec((SB, n, w), lambda i: (sb0 + i, 0, 0))
    vec = full((1, 1024))
    wspec = full((LW // GATE_CH, GATE_CH, GATE_CH))
    kvspec, cspec, lcspec = st3(WIN, KVW), st3(CK - 1, CW), st3(LCK - 1, LW)
    hspec = pl.BlockSpec((SB, LW), lambda i: (sb0 + i, 0))
    anyspec = pl.BlockSpec(memory_space=pl.ANY)
    out_row = pl.BlockSpec((rows, 1024), lambda i: (i, 0))
    dw_w, dw_b, ln_g, ln_b = conv_p
    cw, cb, wr, br, wi, bi, lam = lru_p
    row = lambda v: v.reshape(1, -1)
    blk = pltpu.VMEM((rows, 1024), F32)
    shape_of = lambda a: jax.ShapeDtypeStruct(a.shape, a.dtype)
    n_in = 24
    outs = pl.pallas_call(
        _sample_branches_kernel,
        grid=(DB // SB,),
        in_specs=[pl.BlockSpec(memory_space=pltpu.SMEM),
                  zrow(AW, 0), zrow(KVW, kb), zrow(KVW, vb), kvspec, kvspec,
                  zrow(CW, Z_A // CW), zrow(CW, Z_GATE // CW), cspec, full((CK, CW)), vec, vec, vec,
                  zrow(LW, Z_LX // LW), zrow(LW, Z_LG // LW), lcspec, hspec,
                  full((LCK, LW)), vec, wspec, vec, wspec, vec, vec] + [anyspec] * 5,
        out_specs=[out_row, kvspec, kvspec, out_row, cspec, out_row, lcspec, hspec],
        out_shape=[jax.ShapeDtypeStruct((MS, AW), BF16), shape_of(states_out[0]), shape_of(states_out[1]),
                   jax.ShapeDtypeStruct((MS, CW), BF16), shape_of(states_out[2]),
                   jax.ShapeDtypeStruct((MS, LW), BF16), shape_of(states_out[3]), shape_of(states_out[4])],
        scratch_shapes=[pltpu.VMEM((SB, CONV_HIST + T, CW), F32), blk,
                        pltpu.VMEM((SB, LRU_HIST + T, LW), F32), blk, blk, blk, blk],
        input_output_aliases={n_in: 1, n_in + 1: 2, n_in + 2: 4, n_in + 3: 6, n_in + 4: 7},
        compiler_params=_cp("parallel"),
        name="sample_branches",
    )(sinks, z, z, z, states_in[0], states_in[1], z, z, states_in[2], dw_w, row(dw_b), row(ln_g), row(ln_b),
      z, z, states_in[3], states_in[4], cw, row(cb), wr, row(br), wi, row(bi), row(lam), *states_out)
    o, nk, nv, c, nst, rr, ncst, nh = outs
    return o, c, rr, (nk, nv, nst, ncst, nh)


ROUTE_W = 128


def _mix_body(branch_refs, gates, x_ref, wa_ref, wc_ref, wl_ref, wm_ref, g_ref, x1_ref):
    is_prompt = pl.program_id(0) < MP // TM_TOK
    o, c, r = (jnp.where(is_prompt, p_ref[...], s_ref[...]) for p_ref, s_ref in branch_refs)
    half = D // 2
    parts = []
    for hh in range(2):
        sl = slice(hh * half, (hh + 1) * half)
        m = _sigmoid(gates[0][hh][...].astype(F32)) * jnp.dot(o, wa_ref[:, sl], preferred_element_type=F32)
        m = m + _sigmoid(gates[1][hh][...].astype(F32)) * jnp.dot(c, wc_ref[:, sl], preferred_element_type=F32)
        m = m + _sigmoid(gates[2][hh][...].astype(F32)) * jnp.dot(r, wl_ref[:, sl], preferred_element_type=F32)
        parts.append(m.astype(BF16))
    merged = jnp.concatenate(parts, axis=1)
    x1 = x_ref[...] + jnp.dot(merged, wm_ref[...], preferred_element_type=F32)
    x1_ref[...] = x1
    return _rms(x1, g_ref[...])


def _mix_kernel(op_ref, os_ref, cp_ref, cs_ref, rp_ref, rs_ref, ga0, ga1, gb0, gb1, gc0, gc1, x_ref,
                wa_ref, wc_ref, wl_ref, wm_ref, g_ref, x1_ref, hn_ref):
    hn = _mix_body(((op_ref, os_ref), (cp_ref, cs_ref), (rp_ref, rs_ref)), ((ga0, ga1), (gb0, gb1), (gc0, gc1)),
                   x_ref, wa_ref, wc_ref, wl_ref, wm_ref, g_ref, x1_ref)
    hn_ref[...] = hn.astype(hn_ref.dtype)


def _mix_router_kernel(op_ref, os_ref, cp_ref, cs_ref, rp_ref, rs_ref, ga0, ga1, gb0, gb1, gc0, gc1, x_ref,
                       wa_ref, wc_ref, wl_ref, wm_ref, g_ref, rw_ref, rb_ref, x1_ref, route_ref):
    hn = _mix_body(((op_ref, os_ref), (cp_ref, cs_ref), (rp_ref, rs_ref)), ((ga0, ga1), (gb0, gb1), (gc0, gc1)),
                   x_ref, wa_ref, wc_ref, wl_ref, wm_ref, g_ref, x1_ref)
    h_hi = hn.astype(BF16)
    h_lo = (hn - h_hi.astype(F32)).astype(BF16)
    both = jnp.dot(h_hi, rw_ref[...], preferred_element_type=F32)
    logits = (both[:, :ROUTE_W] + both[:, ROUTE_W:]
              + jnp.dot(h_lo, rw_ref[:, :ROUTE_W], preferred_element_type=F32) + rb_ref[...])
    lane = lax.broadcasted_iota(jnp.int32, logits.shape, 1)
    ninf = -jnp.inf
    l1 = jnp.where(lane < NE, logits, ninf)
    m1 = jnp.max(l1, axis=-1, keepdims=True)
    i1 = jnp.min(jnp.where(l1 == m1, lane, ROUTE_W), axis=-1, keepdims=True)
    l2 = jnp.where(lane == i1, ninf, l1)
    m2 = jnp.max(l2, axis=-1, keepdims=True)
    i2 = jnp.min(jnp.where(l2 == m2, lane, ROUTE_W), axis=-1, keepdims=True)
    e2 = jnp.exp(m2 - m1)
    den = 1.0 + e2
    out = jnp.where(lane == 0, i1.astype(F32), 0.0)
    out = jnp.where(lane == 1, i2.astype(F32), out)
    out = jnp.where(lane == 2, 1.0 / den, out)
    out = jnp.where(lane == 3, e2 / den, out)
    route_ref[...] = out


def mix(o, c, rr, z, x, wa, wc, wl, wm, g, layer, router=None):
    tm = TM_TOK
    half = D // 2
    npt = MP // tm
    row = lambda col: (lambda i: (i, col))
    const = lambda shape: pl.BlockSpec(shape, lambda i: (0,) * len(shape), pipeline_mode=pl.Buffered(1))
    wspec = lambda k: pl.BlockSpec((None, k, D), lambda i: (layer, 0, 0), pipeline_mode=pl.Buffered(1))
    gate_specs = [pl.BlockSpec((tm, half), row(Z_GA // half + k)) for k in range(6)]
    pair = lambda w: [pl.BlockSpec((tm, w), lambda i: (jnp.minimum(i, npt - 1), 0)),
                      pl.BlockSpec((tm, w), lambda i: (jnp.maximum(i - npt, 0), 0))]
    in_specs = (pair(AW) + pair(CW) + pair(LW) + gate_specs
                + [pl.BlockSpec((tm, D), row(0)), wspec(AW), wspec(CW), wspec(LW), wspec(D), const((1, D))])
    args = [*o, *c, *rr, z, z, z, z, z, z, x, wa, wc, wl, wm, g.reshape(1, D)]
    if router is None:
        kern = _mix_kernel
        out_specs = [pl.BlockSpec((tm, D), row(0)), pl.BlockSpec((tm, D), row(0))]
        out_shape = [jax.ShapeDtypeStruct((M, D), F32), jax.ShapeDtypeStruct((M, D), BF16)]
    else:
        kern = _mix_router_kernel
        rw, rb = router
        in_specs += [const((D, 2 * ROUTE_W)), const((1, ROUTE_W))]
        args += [rw, rb]
        out_specs = [pl.BlockSpec((tm, D), row(0)), pl.BlockSpec((tm, ROUTE_W), row(0))]
        out_shape = [jax.ShapeDtypeStruct((M, D), F32), jax.ShapeDtypeStruct((M, ROUTE_W), F32)]
    return pl.pallas_call(
        kern,
        grid=(M // tm,),
        in_specs=in_specs,
        out_specs=out_specs,
        out_shape=out_shape,
        compiler_params=_cp("parallel"),
        name="mix",
    )(*args)


def _ffn_up_kernel(x_ref, w1_ref, w3_ref, *refs, plan):
    n = len(plan)
    o_ref = refs[n]
    x = x_ref[...]
    a = jnp.dot(x, w1_ref[...], preferred_element_type=F32)
    b = jnp.dot(x, w3_ref[...], preferred_element_type=F32)
    o_ref[...] = (_silu(a) * b).astype(o_ref.dtype)
    _run_side(plan, refs[:n], refs[n + 1:], pl.program_id(0) * pl.num_programs(1) + pl.program_id(1))


def ffn_up(hn, w1, w3, streams):
    tm, tn = TM_BIG, 512
    nj = DFF // tn
    wspec = pl.BlockSpec((D, tn), lambda i, j: (0, j))
    plan, s_in, s_out, s_shapes = _side_plan(streams, lambda i, j: i * nj + j, (M // tm) * nj)
    outs = pl.pallas_call(
        functools.partial(_ffn_up_kernel, plan=plan),
        grid=(M // tm, nj),
        in_specs=[pl.BlockSpec((tm, D), lambda i, j: (i, 0)), wspec, wspec] + s_in,
        out_specs=[pl.BlockSpec((tm, tn), lambda i, j: (i, j))] + s_out,
        out_shape=[jax.ShapeDtypeStruct((M, DFF), BF16)] + s_shapes,
        compiler_params=_cp("arbitrary", "arbitrary"),
        name="ffn_up",
    )(hn, w1, w3, *[s[0] for s in streams])
    return outs[0], outs[1:]


def _ffn_down_kernel(h_ref, w_ref, x_ref, g_ref, *refs, plan):
    n = len(plan)
    x2_ref, xn_ref = refs[n], refs[n + 1]
    x2 = x_ref[...] + jnp.dot(h_ref[...], w_ref[...], preferred_element_type=F32)
    x2_ref[...] = x2
    xn_ref[...] = _rms(x2, g_ref[...]).astype(xn_ref.dtype)
    _run_side(plan, refs[:n], refs[n + 2:], pl.program_id(0))


def ffn_down(h, w2, x1, g, streams):
    tm = TM_TOK
    plan, s_in, s_out, s_shapes = _side_plan(streams, lambda i: i, M // tm)
    outs = pl.pallas_call(
        functools.partial(_ffn_down_kernel, plan=plan),
        grid=(M // tm,),
        in_specs=[pl.BlockSpec((tm, DFF), lambda i: (i, 0)),
                  pl.BlockSpec((DFF, D), lambda i: (0, 0), pipeline_mode=pl.Buffered(1)),
                  pl.BlockSpec((tm, D), lambda i: (i, 0)), pl.BlockSpec((1, D), lambda i: (0, 0))] + s_in,
        out_specs=[pl.BlockSpec((tm, D), lambda i: (i, 0)), pl.BlockSpec((tm, D), lambda i: (i, 0))] + s_out,
        out_shape=[jax.ShapeDtypeStruct((M, D), F32), jax.ShapeDtypeStruct((M, D), BF16)] + s_shapes,
        compiler_params=_cp("arbitrary"),
        name="ffn_down",
    )(h, w2, x1, g.reshape(1, D), *[s[0] for s in streams])
    return outs[0], outs[1], outs[2:]


def _row_copy(src, src_row, dst, dst_row, sem):
    return pltpu.make_async_copy(src.at[pl.ds(src_row, 1)], dst.at[pl.ds(dst_row, 1)], sem)


def _moe_scatter_kernel(s1_ref, s2_ref, zt_ref, x_ref, g_ref, xs_ref, buf_ref, zero_ref, sem):
    i = pl.program_id(0)
    tm = MOE_TM
    slot = i % 2

    def tile_copy(e):
        return pltpu.make_async_copy(zero_ref, xs_ref.at[pl.ds(pl.multiple_of(zt_ref[e] * tm, tm), tm)],
                                     sem.at[0])

    @pl.when(i == 0)
    def _():
        zero_ref[...] = jnp.zeros(zero_ref.shape, F32)
        for e in range(2 * NE):
            tile_copy(e).start()
            tile_copy(e).wait()

    buf_ref[slot] = _rms(x_ref[...], g_ref[...])

    def issue(r, carry):
        t = i * tm + r
        _row_copy(buf_ref.at[slot], r, xs_ref, s1_ref[t], sem.at[slot]).start()
        _row_copy(buf_ref.at[slot], r, xs_ref, s2_ref[t], sem.at[slot]).start()
        return carry

    lax.fori_loop(0, tm, issue, 0, unroll=8)

    def drain_slot(sl):
        def drain(r, carry):
            _row_copy(buf_ref.at[sl], 0, xs_ref, 0, sem.at[sl]).wait()
            _row_copy(buf_ref.at[sl], 0, xs_ref, 0, sem.at[sl]).wait()
            return carry

        lax.fori_loop(0, tm, drain, 0, unroll=8)

    @pl.when(i > 0)
    def _():
        drain_slot(1 - slot)

    @pl.when(i == pl.num_programs(0) - 1)
    def _():
        drain_slot(slot)


def moe_scatter(x1, g, slot1, slot2, zero_tiles):
    tm = MOE_TM
    return pl.pallas_call(
        _moe_scatter_kernel,
        grid_spec=pltpu.PrefetchScalarGridSpec(
            num_scalar_prefetch=3,
            grid=(M // tm,),
            in_specs=[pl.BlockSpec((tm, D), lambda i, *_: (i, 0)), pl.BlockSpec((1, D), lambda i, *_: (0, 0))],
            out_specs=pl.BlockSpec(memory_space=pl.ANY),
            scratch_shapes=[pltpu.VMEM((2, tm, D), F32), pltpu.VMEM((tm, D), F32),
                            pltpu.SemaphoreType.DMA((2,))]),
        out_shape=jax.ShapeDtypeStruct((MOE_ROWS + 2 * MOE_SKIP, D), F32),
        compiler_params=_cp("arbitrary"),
        name="moe_scatter",
    )(slot1, slot2, zero_tiles, x1, g.reshape(1, D))


def _moe_up_kernel(te_ref, nu_ref, x_ref, w1_ref, w3_ref, o_ref):
    used = pl.program_id(1) < nu_ref[0]

    @pl.when(used)
    def _():
        x = x_ref[...].astype(BF16)
        a = jnp.dot(x, w1_ref[0], preferred_element_type=F32)
        b = jnp.dot(x, w3_ref[0], preferred_element_type=F32)
        o_ref[...] = (_silu(a) * b).astype(o_ref.dtype)

    @pl.when(jnp.logical_not(used))
    def _():
        o_ref[...] = jnp.zeros(o_ref.shape, o_ref.dtype)


def _used(i, nu_ref):
    return jnp.minimum(i, nu_ref[0] - 1)


def moe_up(xs, w1, w3, tile_expert, n_used, idx):
    tm, tn = MOE_TM, UP_TN
    wspec = pl.BlockSpec((None, 1, D, tn), lambda j, i, te, nu: (idx, te[_used(i, nu)], 0, j))
    return pl.pallas_call(
        _moe_up_kernel,
        grid_spec=pltpu.PrefetchScalarGridSpec(
            num_scalar_prefetch=2,
            grid=(DFF // tn, MOE_TILES),
            in_specs=[pl.BlockSpec((tm, D), lambda j, i, te, nu: (_used(i, nu), 0)), wspec, wspec],
            out_specs=pl.BlockSpec((tm, tn), lambda j, i, te, nu: (i, j))),
        out_shape=jax.ShapeDtypeStruct((MOE_ROWS, DFF), BF16),
        compiler_params=_cp("arbitrary", "arbitrary"),
        name="moe_up",
    )(tile_expert, n_used, xs, w1, w3)


def _moe_down_kernel(te_ref, nu_ref, h_ref, w_ref, o_ref):
    used = pl.program_id(1) < nu_ref[0]

    @pl.when(used)
    def _():
        o_ref[...] = jnp.dot(h_ref[...], w_ref[0], preferred_element_type=F32)

    @pl.when(jnp.logical_not(used))
    def _():
        o_ref[...] = jnp.zeros(o_ref.shape, o_ref.dtype)


def moe_down(hs, w2, tile_expert, n_used, idx):
    tm, tn = MOE_TM, DOWN_TN
    return pl.pallas_call(
        _moe_down_kernel,
        grid_spec=pltpu.PrefetchScalarGridSpec(
            num_scalar_prefetch=2,
            grid=(D // tn, MOE_TILES),
            in_specs=[pl.BlockSpec((tm, DFF), lambda j, i, te, nu: (_used(i, nu), 0)),
                      pl.BlockSpec((None, 1, DFF, tn), lambda j, i, te, nu: (idx, te[_used(i, nu)], 0, j))],
            out_specs=pl.BlockSpec((tm, tn), lambda j, i, te, nu: (i, j))),
        out_shape=jax.ShapeDtypeStruct((MOE_ROWS, D), F32),
        compiler_params=_cp("arbitrary", "arbitrary"),
        name="moe_down",
    )(tile_expert, n_used, hs, w2)


COMB_TM = BLK


def _moe_combine_kernel(s1_ref, s2_ref, x_ref, route_ref, g_ref, ys_ref, yp_ref, ysm_ref, ya_ref, yb_ref, sem):
    i = pl.program_id(0)
    tm = COMB_TM
    slot = i % 2

    def fetch(tile, sl):
        def issue(r, carry):
            t = tile * tm + r
            _row_copy(ys_ref, s1_ref[t], ya_ref.at[sl], r, sem.at[sl]).start()
            _row_copy(ys_ref, s2_ref[t], yb_ref.at[sl], r, sem.at[sl]).start()
            return carry

        lax.fori_loop(0, tm, issue, 0, unroll=8)

    @pl.when(i == 0)
    def _():
        fetch(0, 0)

    @pl.when(i + 1 < pl.num_programs(0))
    def _():
        fetch(i + 1, 1 - slot)

    def drain(r, carry):
        _row_copy(ys_ref, 0, ya_ref.at[slot], 0, sem.at[slot]).wait()
        _row_copy(ys_ref, 0, yb_ref.at[slot], 0, sem.at[slot]).wait()
        return carry

    lax.fori_loop(0, tm, drain, 0, unroll=8)
    w1 = route_ref[:, 2:3]
    w2 = route_ref[:, 3:4]
    x2 = x_ref[...] + (w1 * ya_ref[slot] + w2 * yb_ref[slot])
    y = _rms(x2, g_ref[...])
    is_prompt = i < B * NB

    @pl.when(is_prompt & (i % NB > 0))
    def _():
        yp_ref[...] = y

    @pl.when(jnp.logical_not(is_prompt))
    def _():
        ysm_ref[...] = y


def moe_combine(x1, route, g, ys, slot1, slot2):
    tm = COMB_TM
    return pl.pallas_call(
        _moe_combine_kernel,
        grid_spec=pltpu.PrefetchScalarGridSpec(
            num_scalar_prefetch=2,
            grid=(M // tm,),
            in_specs=[pl.BlockSpec((tm, D), lambda i, *_: (i, 0)), pl.BlockSpec((tm, ROUTE_W), lambda i, *_: (i, 0)),
                      pl.BlockSpec((1, D), lambda i, *_: (0, 0)), pl.BlockSpec(memory_space=pl.ANY)],
            out_specs=[pl.BlockSpec((tm, D), lambda i, *_: (_prompt_block(i), 0)),
                       pl.BlockSpec((tm, D), lambda i, *_: (jnp.maximum(i - B * NB, 0), 0))],
            scratch_shapes=[pltpu.VMEM((2, tm, D), F32), pltpu.VMEM((2, tm, D), F32),
                            pltpu.SemaphoreType.DMA((2,))]),
        out_shape=[jax.ShapeDtypeStruct((B * SEQ, D), F32), jax.ShapeDtypeStruct((MS, D), F32)],
        compiler_params=_cp("arbitrary"),
        name="moe_combine",
    )(slot1, slot2, x1, route, g.reshape(1, D), ys)


def _moe_plan(route):
    e1 = route[:, 0].astype(jnp.int32)
    e2 = route[:, 1].astype(jnp.int32)
    ids = jnp.arange(NE, dtype=jnp.int32)
    r = jnp.arange(M, dtype=jnp.int32)
    skip = (r < MP) & (r % PB < BLK)
    spare = (r // PB) * BLK + r % PB
    sel = (e1[:, None] == ids).astype(jnp.int32) + (e2[:, None] == ids).astype(jnp.int32)
    sel = jnp.where(skip[:, None], 0, sel)
    incl = jnp.cumsum(sel, axis=0)
    rank = incl - sel
    cnt = incl[-1]
    ntile = (cnt + MOE_TM - 1) // MOE_TM
    tile_end = jnp.cumsum(ntile)
    tile_off = tile_end - ntile
    row_off = tile_off * MOE_TM
    slot1 = jnp.take(row_off, e1) + jnp.take_along_axis(rank, e1[:, None], axis=1)[:, 0]
    slot2 = jnp.take(row_off, e2) + jnp.take_along_axis(rank, e2[:, None], axis=1)[:, 0]
    tiles = jnp.arange(MOE_TILES, dtype=jnp.int32)
    tile_expert = jnp.minimum(jnp.sum((tiles[:, None] >= tile_end[None, :]).astype(jnp.int32), axis=1), NE - 1)
    n_used = tile_end[-1:].astype(jnp.int32)
    tail = jnp.minimum(tile_end[-1] + ids, MOE_TILES - 1)
    zero_tiles = jnp.concatenate([jnp.maximum(tile_end - 1, 0), tail]).astype(jnp.int32)
    scatter = (jnp.where(skip, MOE_ROWS + spare, slot1).astype(jnp.int32),
               jnp.where(skip, MOE_ROWS + MOE_SKIP + spare, slot2).astype(jnp.int32))
    gather = (jnp.where(skip, spare, slot1).astype(jnp.int32), jnp.where(skip, spare, slot2).astype(jnp.int32))
    return scatter, gather, tile_expert.astype(jnp.int32), n_used, zero_tiles


def _blockdiag(w):
    per = GATE_CH // (LW // 16)
    w4 = w.reshape(LW // GATE_CH, per, 64, 64)
    eye = jnp.eye(per, dtype=w.dtype)
    return jnp.einsum("cnde,nm->cndme", w4, eye).reshape(LW // GATE_CH, GATE_CH, GATE_CH).astype(BF16)


def kernel(x_prompt, x_sample, cache_attn_k, cache_attn_v, state_conv, state_lru_conv, state_lru_h, meta_tokens,
           norm_mix, norm_ffn, norm_final, w_in, attn_sinks, w_attn_out, conv_dw_w, conv_dw_b, conv_ln_g, conv_ln_b,
           w_conv_out, lru_conv_w, lru_conv_b, lru_w_r, lru_b_r, lru_w_i, lru_b_i, lru_lambda, w_lru_out, w_mix_out,
           ffn_w1, ffn_w3, ffn_w2, moe_router_w, moe_router_b, moe_w1, moe_w3, moe_w2):
    states_in = (cache_attn_k.reshape(DEPTH * DB, WIN, KVW), cache_attn_v.reshape(DEPTH * DB, WIN, KVW),
                 state_conv.reshape(DEPTH * DB, CK - 1, CW), state_lru_conv.reshape(DEPTH * DB, LCK - 1, LW),
                 state_lru_h.reshape(DEPTH * DB, LW))
    states_out = tuple(jnp.zeros(s.shape, F32) for s in states_in)
    p_k, p_v, p_conv, p_lconv, p_h = [], [], [], [], []

    assert (DEPTH, ffn_w1.shape[0], moe_w1.shape[0]) == (2, 1, 1), \
        "layer 0: dense SwiGLU; layer 1: routed experts followed by the final norm"
    wa, wc, wl, wm = (w.astype(BF16) for w in (w_attn_out, w_conv_out, w_lru_out, w_mix_out))
    w_in0 = w_in[0].astype(BF16)

    def branches(l, z):
        nonlocal states_out
        sinks = attn_sinks[l]
        conv_p = (conv_dw_w[l], conv_dw_b[l], conv_ln_g[l], conv_ln_b[l])
        lru_p = (lru_conv_w[l], lru_conv_b[l], _blockdiag(lru_w_r[l]), lru_b_r[l], _blockdiag(lru_w_i[l]),
                 lru_b_i[l], lru_lambda[l])
        o_p, c_p, rr_p, pc, plc, ph = prompt_branches(z, sinks, conv_p, lru_p)
        o_s, c_s, rr_s, states_out = sample_branches(z, sinks, conv_p, lru_p, states_in, states_out, l)
        kv_last = jnp.stack([lax.slice(z, ((b + 1) * PB - WIN, Z_K), ((b + 1) * PB, Z_K + 2 * KVW))
                             for b in range(B)]).astype(F32)
        p_k.append(kv_last[:, :, :KVW].reshape(B, WIN, NKV, HD))
        p_v.append(kv_last[:, :, KVW:].reshape(B, WIN, NKV, HD))
        p_conv.append(pc)
        p_lconv.append(plc)
        p_h.append(ph.reshape(B, LW))
        return (o_p, o_s), (c_p, c_s), (rr_p, rr_s)

    x, xn = embed_norm(x_prompt, x_sample, meta_tokens, norm_mix[0])

    z, (moe_w1_b, ffn_w1_b, ffn_w3_b, ffn_w2_b) = inproj(
        xn, w_in0, [(moe_w1.reshape(-1, DFF), 128, None), (ffn_w1, 128, 0), (ffn_w3, 128, 0), (ffn_w2, 256, 0)])
    o, c, rr = branches(0, z)
    x1, hn = mix(o, c, rr, z, x, wa, wc, wl, wm, norm_ffn[0], 0)
    h, (moe_w2_b,) = ffn_up(hn, ffn_w1_b, ffn_w3_b, [(moe_w2.reshape(-1, D), 512, None)])
    x, xn, (w_in1,) = ffn_down(h, ffn_w2_b, x1, norm_mix[1], [(w_in, 64, 1)])

    z, (moe_w3_b,) = inproj(xn, w_in1, [(moe_w3.reshape(-1, DFF), 256, None)], tm=2 * TM_BIG)
    o, c, rr = branches(1, z)
    rw = jnp.pad(moe_router_w[0], ((0, 0), (0, ROUTE_W - NE)))
    rw_hi = rw.astype(BF16)
    rw_lo = (rw - rw_hi.astype(F32)).astype(BF16)
    rb = jnp.pad(moe_router_b[0].reshape(1, NE), ((0, 0), (0, ROUTE_W - NE)))
    x1, route = mix(o, c, rr, z, x, wa, wc, wl, wm, norm_ffn[1], 1,
                    router=(jnp.concatenate([rw_hi, rw_lo], axis=1), rb))
    scatter_slots, gather_slots, tile_expert, n_used, zero_tiles = _moe_plan(route)
    xs = moe_scatter(x1, norm_ffn[1], *scatter_slots, zero_tiles)
    hs = moe_up(xs, moe_w1_b.reshape(moe_w1.shape), moe_w3_b.reshape(moe_w3.shape), tile_expert, n_used, 0)
    ys = moe_down(hs, moe_w2_b.reshape(moe_w2.shape), tile_expert, n_used, 0)
    y_prompt, y_sample = moe_combine(x1, route, norm_final, ys, *gather_slots)

    y_prompt = y_prompt.reshape(B, SEQ, D)
    y_sample = y_sample.reshape(DB, T, D)
    s_k, s_v, s_conv, s_lconv, s_h = states_out
    st4 = lambda a: a.reshape(DEPTH, DB, WIN, NKV, HD)
    return (y_prompt, y_sample,
            jnp.stack(p_k), jnp.stack(p_v), jnp.stack(p_conv), jnp.stack(p_lconv), jnp.stack(p_h),
            st4(s_k), st4(s_v), s_conv.reshape(DEPTH, DB, CK - 1, CW), s_lconv.reshape(DEPTH, DB, LCK - 1, LW),
            s_h.reshape(DEPTH, DB, LW))
```
